```python
import math
import jax, jax.numpy as jnp
from jax import lax
import numpy as np

D_MODEL = 4096
BATCH = 2
SEQ = 8192
DEPTH = 2

N_MEM = 256
DA_HEAD_DIM = 128
DA_HEADS = D_MODEL // (2 * DA_HEAD_DIM)
Q_BLOCK = 128
N_BUCKETS = 32
MAX_DIST = 128
RW_HEAD = 64
RW_HEADS = D_MODEL // RW_HEAD
RW_DECAY_LORA = max(32, int(round(1.8 * D_MODEL ** 0.5 / 32)) * 32)
RW_AAA_LORA = max(32, int(round(1.8 * D_MODEL ** 0.5 / 32)) * 32)
RW_GATE_LORA = max(32, int(round(0.6 * D_MODEL ** 0.8 / 32)) * 32)
GN_EPS = 64e-5
CA_HEADS = 4
CA_HEAD_DIM = 128
CA_DIM = CA_HEADS * CA_HEAD_DIM
D_FF = ((8 * D_MODEL // 3 + 255) // 256) * 256
CONV_W = 3
N_A = (DEPTH + 1) // 2
N_B = DEPTH // 2
ALPHA = (2 * DEPTH) ** 0.25
BETA = (8 * DEPTH) ** -0.25
LN_EPS = 1e-5

kernel_name = "hybrid_diffattn_rwkv7_memxattn_convglu"


def layer_norm(x, g, b):
    xf = x.astype(jnp.float32)
    mu = jnp.mean(xf, axis=-1, keepdims=True)
    var = jnp.mean(jnp.square(xf - mu), axis=-1, keepdims=True)
    return ((xf - mu) * lax.rsqrt(var + LN_EPS) * g.astype(jnp.float32) + b.astype(jnp.float32)).astype(x.dtype)


def t5_bucket(n):
    max_exact = N_BUCKETS // 2
    nf = jnp.maximum(n, 1).astype(jnp.float32)
    large = max_exact + (jnp.log(nf / max_exact) / math.log(MAX_DIST / max_exact)
                         * (N_BUCKETS - max_exact)).astype(jnp.int32)
    large = jnp.minimum(large, N_BUCKETS - 1)
    return jnp.where(n < max_exact, n, large)


def diff_attention(x, positions, rel_bias, w_qkv, lam_vecs, subln_w, w_o, lam_init):
    B, S, D = x.shape
    H, d = DA_HEADS, DA_HEAD_DIM
    nb = S // Q_BLOCK
    qkv = x @ w_qkv
    q, k, v = jnp.split(qkv, 3, axis=-1)
    q = q.reshape(B, S, H, 2, d) * (d ** -0.5)
    k = k.reshape(B, S, H, 2, d)
    v = v.reshape(B, S, H, 2 * d).transpose(0, 2, 1, 3)
    q1 = q[..., 0, :].transpose(0, 2, 1, 3)
    q2 = q[..., 1, :].transpose(0, 2, 1, 3)
    k1 = k[..., 0, :].transpose(0, 2, 1, 3)
    k2 = k[..., 1, :].transpose(0, 2, 1, 3)
    lv = lam_vecs.astype(jnp.float32)
    lam = jnp.exp(jnp.sum(lv[0] * lv[1])) - jnp.exp(jnp.sum(lv[2] * lv[3])) + lam_init
    bias_table = rel_bias.astype(jnp.float32).T
    neg = jnp.finfo(jnp.float32).min

    def to_blocks(t):
        return jnp.moveaxis(t.reshape(B, H, nb, Q_BLOCK, t.shape[-1]), 2, 0)

    qpos = jnp.moveaxis(positions.reshape(B, nb, Q_BLOCK), 1, 0)

    def block(args):
        q1b, q2b, qp = args
        n = qp[:, :, None] - positions[:, None, :]
        mask = (n >= 0)[:, None]
        bias = jnp.take(bias_table, t5_bucket(jnp.maximum(n, 0)), axis=1).transpose(1, 0, 2, 3)

        def probs(qb, kb):
            s = jnp.einsum('bhqd,bhkd->bhqk', qb, kb).astype(jnp.float32) + bias
            return jax.nn.softmax(jnp.where(mask, s, neg), axis=-1)

        a = probs(q1b, k1) - lam * probs(q2b, k2)
        return jnp.einsum('bhqk,bhkv->bhqv', a.astype(v.dtype), v)

    o = lax.map(block, (to_blocks(q1), to_blocks(q2), qpos))
    o = o.transpose(1, 0, 3, 2, 4).reshape(B, S, H, 2 * d).astype(jnp.float32)
    o = o * lax.rsqrt(jnp.mean(jnp.square(o), axis=-1, keepdims=True) + LN_EPS)
    o = o * subln_w.astype(jnp.float32) * (1.0 - lam_init)
    return o.reshape(B, S, D).astype(x.dtype) @ w_o


def rwkv7_step(state, inp):
    r_t, w_t, k_t, v_t, a_t, b_t = inp
    sa = jnp.einsum('bhvk,bhk->bhv', state, a_t)
    state = (state * w_t[:, :, None, :] + sa[..., None] * b_t[:, :, None, :]
             + v_t[..., None] * k_t[:, :, None, :])
    y = jnp.einsum('bhvk,bhk->bhv', state, r_t)
    return state, y


def rwkv7_time_mix(x, mix, w_rkv, w0, w1, w2, a0, a1, a2, g1, g2, k_k, k_a, r_k, lnx_w, lnx_b, w_o):
    B, S, D = x.shape
    H, N = RW_HEADS, RW_HEAD
    f32 = jnp.float32
    xx = jnp.pad(x, ((0, 0), (1, 0), (0, 0)))[:, :-1] - x
    xr = x + xx * mix[0]
    xw = x + xx * mix[1]
    xk = x + xx * mix[2]
    xv = x + xx * mix[3]
    xa = x + xx * mix[4]
    xg = x + xx * mix[5]
    r = (xr @ w_rkv[0]).astype(f32)
    k = (xk @ w_rkv[1]).astype(f32)
    v = (xv @ w_rkv[2]).astype(f32)
    w_log = -jax.nn.softplus(-(w0 + jnp.tanh(xw @ w1) @ w2).astype(f32)) - 0.5
    decay = jnp.exp(-jnp.exp(w_log))
    a = jax.nn.sigmoid((a0 + (xa @ a1) @ a2).astype(f32))
    g = (jax.nn.sigmoid(xg @ g1) @ g2).astype(f32)
    heads = lambda t: t.reshape(B, S, H, N)
    kk = heads(k * k_k.astype(f32))
    kk = kk / jnp.maximum(jnp.linalg.norm(kk, axis=-1, keepdims=True), 1e-12)
    k = k * (1.0 + (a - 1.0) * k_a.astype(f32))
    seq_first = lambda t: jnp.moveaxis(t, 1, 0)
    inputs = (seq_first(heads(r)), seq_first(heads(decay)), seq_first(heads(k)),
              seq_first(heads(v)), seq_first(-kk), seq_first(kk * heads(a)))
    state0 = jnp.zeros((B, H, N, N), f32)
    _, y = lax.scan(rwkv7_step, state0, inputs)
    y = jnp.moveaxis(y, 0, 1)
    mu = jnp.mean(y, axis=-1, keepdims=True)
    var = jnp.mean(jnp.square(y - mu), axis=-1, keepdims=True)
    yn = ((y - mu) * lax.rsqrt(var + GN_EPS)).reshape(B, S, D) * lnx_w.astype(f32) + lnx_b.astype(f32)
    bonus = jnp.sum(heads(r) * heads(k) * r_k.astype(f32), axis=-1, keepdims=True) * heads(v)
    out = (yn + bonus.reshape(B, S, D)) * g
    return out.astype(x.dtype) @ w_o


def memory_cross_attention(x, mem, w_q, w_kv, w_o):
    B, S, _ = x.shape
    M = mem.shape[1]
    q = (x @ w_q).reshape(B, S, CA_HEADS, CA_HEAD_DIM) * (CA_HEAD_DIM ** -0.5)
    kv = (mem @ w_kv).reshape(B, M, 2, CA_HEADS, CA_HEAD_DIM)
    s = jnp.einsum('bshd,bmhd->bhsm', q, kv[:, :, 0]).astype(jnp.float32)
    p = jax.nn.softmax(s, axis=-1)
    o = jnp.einsum('bhsm,bmhd->bshd', p.astype(x.dtype), kv[:, :, 1]).reshape(B, S, CA_DIM)
    return o @ w_o


def conv_glu_ffn(x, w_up, conv_w, conv_b, w_down):
    h = x @ w_up
    C = h.shape[-1]
    h = lax.conv_general_dilated(h, conv_w[:, None, :].astype(h.dtype), window_strides=(1,),
                                 padding=[(CONV_W - 1, 0)], dimension_numbers=('NWC', 'WIO', 'NWC'),
                                 feature_group_count=C) + conv_b
    gate, val = jnp.split(h, 2, axis=-1)
    return (jax.nn.silu(gate) * val) @ w_down


def setup_inputs(seed: int = 0) -> dict:
    key = jax.random.key(seed)
    ks = iter(jax.random.split(key, 40))
    nrm = lambda shape, scale: jax.random.normal(next(ks), shape, jnp.float32) * scale
    uni = lambda shape, lo, hi: jax.random.uniform(next(ks), shape, jnp.float32, lo, hi)
    D, F = D_MODEL, D_FF
    x = nrm((BATCH, SEQ, D), 1.0)
    mem = nrm((BATCH, N_MEM, D), 1.0)
    offset = jax.random.randint(next(ks), (BATCH, 1), 0, 4096, dtype=jnp.int32)
    positions = offset + jnp.arange(SEQ, dtype=jnp.int32)[None, :]
    return {
        'x': x,
        'mem': mem,
        'positions': positions,
        'rel_bias': nrm((N_BUCKETS, DA_HEADS), 0.5),
        'da_w_qkv': nrm((N_A, D, 3 * D), D ** -0.5),
        'da_lam': nrm((N_A, 4, DA_HEAD_DIM), 0.1),
        'da_subln': 1.0 + nrm((N_A, 2 * DA_HEAD_DIM), 0.02),
        'da_w_o': nrm((N_A, D, D), D ** -0.5 * BETA),
        'rw_mix': uni((N_B, 6, D), 0.0, 1.0),
        'rw_w_rkv': nrm((N_B, 3, D, D), D ** -0.5),
        'rw_w0': uni((N_B, D), -6.0, -1.0),
        'rw_w1': nrm((N_B, D, RW_DECAY_LORA), D ** -0.5),
        'rw_w2': nrm((N_B, RW_DECAY_LORA, D), 0.1 * RW_DECAY_LORA ** -0.5),
        'rw_a0': nrm((N_B, D), 0.1),
        'rw_a1': nrm((N_B, D, RW_AAA_LORA), D ** -0.5),
        'rw_a2': nrm((N_B, RW_AAA_LORA, D), 0.1 * RW_AAA_LORA ** -0.5),
        'rw_g1': nrm((N_B, D, RW_GATE_LORA), D ** -0.5),
        'rw_g2': nrm((N_B, RW_GATE_LORA, D), RW_GATE_LORA ** -0.5),
        'rw_k_k': 0.85 + nrm((N_B, D), 0.05),
        'rw_k_a': 1.0 + nrm((N_B, D), 0.05),
        'rw_r_k': nrm((N_B, RW_HEADS, RW_HEAD), 0.1),
        'rw_lnx_w': 1.0 + nrm((N_B, D), 0.02),
        'rw_lnx_b': nrm((N_B, D), 0.02),
        'rw_w_o': nrm((N_B, D, D), D ** -0.5 * BETA),
        'ca_w_q': nrm((DEPTH, D, CA_DIM), D ** -0.5),
        'ca_w_kv': nrm((DEPTH, D, 2 * CA_DIM), D ** -0.5),
        'ca_w_o': nrm((DEPTH, CA_DIM, D), CA_DIM ** -0.5 * BETA),
        'ffn_w_up': nrm((DEPTH, D, 2 * F), D ** -0.5),
        'ffn_conv_w': nrm((DEPTH, CONV_W, 2 * F), CONV_W ** -0.5),
        'ffn_conv_b': nrm((DEPTH, 2 * F), 0.02),
        'ffn_w_down': nrm((DEPTH, F, D), F ** -0.5 * BETA),
        'ln_g': 1.0 + nrm((DEPTH, 3, D), 0.02),
        'ln_b': nrm((DEPTH, 3, D), 0.02),
    }


def reference(x, mem, positions, rel_bias, da_w_qkv, da_lam, da_subln, da_w_o,
              rw_mix, rw_w_rkv, rw_w0, rw_w1, rw_w2, rw_a0, rw_a1, rw_a2, rw_g1, rw_g2,
              rw_k_k, rw_k_a, rw_r_k, rw_lnx_w, rw_lnx_b, rw_w_o,
              ca_w_q, ca_w_kv, ca_w_o, ffn_w_up, ffn_conv_w, ffn_conv_b, ffn_w_down,
              ln_g, ln_b):
    for i in range(DEPTH):
        j = i // 2
        if i % 2 == 0:
            lam_init = 0.8 - 0.6 * math.exp(-0.3 * i)
            h = diff_attention(x, positions, rel_bias, da_w_qkv[j], da_lam[j], da_subln[j], da_w_o[j], lam_init)
        else:
            h = rwkv7_time_mix(x, rw_mix[j], rw_w_rkv[j], rw_w0[j], rw_w1[j], rw_w2[j],
                               rw_a0[j], rw_a1[j], rw_a2[j], rw_g1[j], rw_g2[j],
                               rw_k_k[j], rw_k_a[j], rw_r_k[j], rw_lnx_w[j], rw_lnx_b[j], rw_w_o[j])
        x = layer_norm(ALPHA * x + h, ln_g[i, 0], ln_b[i, 0])
        x = layer_norm(ALPHA * x + memory_cross_attention(x, mem, ca_w_q[i], ca_w_kv[i], ca_w_o[i]),
                       ln_g[i, 1], ln_b[i, 1])
        x = layer_norm(ALPHA * x + conv_glu_ffn(x, ffn_w_up[i], ffn_conv_w[i], ffn_conv_b[i], ffn_w_down[i]),
                       ln_g[i, 2], ln_b[i, 2])
    return x
```

```python
import functools
import math

import jax
import jax.numpy as jnp
from jax import lax
from jax.experimental import pallas as pl
from jax.experimental.pallas import tpu as pltpu

F32 = jnp.float32
BF16 = jnp.bfloat16

_VMEM_LIMIT_BYTES = 56 * 1024 * 1024
_LANES = 128

DA_HEAD_DIM = 128
N_BUCKETS = 32
MAX_DIST = 128
RW_HEAD = 64
GN_EPS = 64e-5
CA_HEADS = 4
CA_HEAD_DIM = 128
CA_DIM = CA_HEADS * CA_HEAD_DIM
CONV_W = 3
LN_EPS = 1e-5
RW_CHUNK = 64

_NEG = float(jnp.finfo(jnp.float32).min)
_HI = lax.Precision.HIGHEST


def _params(sem):
    return pltpu.CompilerParams(dimension_semantics=sem, vmem_limit_bytes=_VMEM_LIMIT_BYTES)


def _pick(n, prefs):
    for p in prefs:
        if n % p == 0:
            return p
    return n


def _mm_kernel(x_ref, w_ref, o_ref, *scratch, nk):
    part = jnp.dot(x_ref[...], w_ref[...], preferred_element_type=F32)
    if nk == 1:
        o_ref[...] = part.astype(o_ref.dtype)
        return
    (acc_ref,) = scratch
    k = pl.program_id(2)

    @pl.when(k == 0)
    def _():
        acc_ref[...] = part

    @pl.when(jnp.logical_and(k > 0, k < nk - 1))
    def _():
        acc_ref[...] += part

    @pl.when(k == nk - 1)
    def _():
        o_ref[...] = (acc_ref[...] + part).astype(o_ref.dtype)


def _matmul(x, w, out_dtype, *, tm=1024, tn=1024, tk=None):
    M, K = x.shape
    N = w.shape[1]
    tm = _pick(M, (tm, 512, 256, 128))
    tn = _pick(N, (tn, 512, 256, 128))
    tk = K if tk is None else tk
    nk = K // tk
    assert K % tk == 0
    scratch = [pltpu.VMEM((tm, tn), F32)] if nk > 1 else []
    return pl.pallas_call(
        functools.partial(_mm_kernel, nk=nk),
        out_shape=jax.ShapeDtypeStruct((M, N), out_dtype),
        grid=(M // tm, N // tn, nk),
        in_specs=[pl.BlockSpec((tm, tk), lambda i, j, k: (i, k)),
                  pl.BlockSpec((tk, tn), lambda i, j, k: (k, j))],
        out_specs=pl.BlockSpec((tm, tn), lambda i, j, k: (i, j)),
        scratch_shapes=scratch,
        compiler_params=_params(("parallel", "parallel", "arbitrary")),
        name="matmul",
    )(x, w)


def _layer_norm(z, g, b):
    mu = jnp.mean(z, axis=-1, keepdims=True)
    zc = z - mu
    var = jnp.mean(zc * zc, axis=-1, keepdims=True)
    return zc * lax.rsqrt(var + LN_EPS) * g + b


def _ln_kernel(x_ref, h_ref, g_ref, b_ref, o_ref, ob_ref, *, alpha):
    y = _layer_norm(alpha * x_ref[...] + h_ref[...], g_ref[...], b_ref[...])
    o_ref[...] = y
    ob_ref[...] = y.astype(BF16)


def _deepnorm(x, h, g, b, alpha):
    M, D = x.shape
    tm = _pick(M, (256, 128))
    row = pl.BlockSpec((tm, D), lambda i: (i, 0))
    vec = pl.BlockSpec((1, D), lambda i: (0, 0))
    return pl.pallas_call(
        functools.partial(_ln_kernel, alpha=alpha),
        out_shape=(jax.ShapeDtypeStruct((M, D), F32), jax.ShapeDtypeStruct((M, D), BF16)),
        grid=(M // tm,),
        in_specs=[row, row, vec, vec],
        out_specs=(row, row),
        compiler_params=_params(("parallel",)),
        name="deepnorm",
    )(x, h, g.reshape(1, D), b.reshape(1, D))


def _t5_thresholds():
    max_exact = N_BUCKETS // 2

    def bucket(n):
        if n < max_exact:
            return n
        val = math.log(n / max_exact) / math.log(MAX_DIST / max_exact) * (N_BUCKETS - max_exact)
        assert n == max_exact or n >= MAX_DIST or abs(val - round(val)) > 1e-3
        return min(max_exact + int(val), N_BUCKETS - 1)

    thr = []
    for j in range(N_BUCKETS):
        n = 0
        while bucket(n) < j:
            n += 1
        thr.append(n)
    return tuple(thr)


def _attn_kernel(qmin_ref, qmax_ref, kmin_ref, kmax_ref, last_ref,
                 q_ref, k_ref, v_ref, qp_ref, kp_ref, tbl_ref, lam_ref, sub_ref,
                 o_ref, m_ref, l_ref, acc_ref, *, tk, thr, lam_init):
    b = pl.program_id(0)
    h = pl.program_id(1)
    i = pl.program_id(2)
    d = DA_HEAD_DIM
    scale = d ** -0.5
    tq = q_ref.shape[0]
    qmn = qmin_ref[b, i]
    qmx = qmax_ref[b, i]

    m_ref[...] = jnp.full(m_ref.shape, _NEG, F32)
    l_ref[...] = jnp.zeros(l_ref.shape, F32)
    acc_ref[...] = jnp.zeros(acc_ref.shape, F32)

    def step(j, near):
        r0 = pl.multiple_of(j * tk, tk)
        kblk = k_ref[pl.ds(r0, tk), :]
        vblk = v_ref[pl.ds(r0, tk), :]
        if near:
            n = qp_ref[...] - kp_ref[:, pl.ds(r0, tk)]
            bias = jnp.full((tq, tk), tbl_ref[h, 0], F32)
            for jj in range(1, N_BUCKETS):
                bias = jnp.where(n >= thr[jj], tbl_ref[h, jj], bias)
            keep = n >= 0
        else:
            bias = tbl_ref[h, N_BUCKETS - 1]
        for mi in range(2):
            q = q_ref[:, mi * d:(mi + 1) * d]
            s = lax.dot_general(q, kblk[:, mi * d:(mi + 1) * d], (((1,), (1,)), ((), ())),
                                preferred_element_type=F32) * scale + bias
            if near:
                s = jnp.where(keep, s, _NEG)
            m_prev = m_ref[mi]
            m_new = jnp.maximum(m_prev, jnp.max(s, axis=-1, keepdims=True))
            alpha = jnp.exp(m_prev - m_new)
            p = jnp.exp(s - m_new)
            l_ref[mi] = alpha * l_ref[mi] + jnp.sum(p, axis=-1, keepdims=True)
            acc_ref[mi] = alpha * acc_ref[mi] + jnp.dot(p.astype(BF16), vblk,
                                                        preferred_element_type=F32)
            m_ref[mi] = m_new

    def body(j, carry):
        active = kmin_ref[b, j] <= qmx
        far = (qmn - kmax_ref[b, j]) >= thr[N_BUCKETS - 1]

        @pl.when(jnp.logical_and(active, far))
        def _():
            step(j, False)

        @pl.when(jnp.logical_and(active, jnp.logical_not(far)))
        def _():
            step(j, True)

        return carry

    lax.fori_loop(0, last_ref[b, i] + 1, body, 0)

    lv = lam_ref[...]
    lam = (jnp.exp(jnp.sum(lv[0:1] * lv[1:2], axis=-1, keepdims=True))
           - jnp.exp(jnp.sum(lv[2:3] * lv[3:4], axis=-1, keepdims=True)) + lam_init)
    o = acc_ref[0] / l_ref[0] - lam * (acc_ref[1] / l_ref[1])
    o = o * lax.rsqrt(jnp.mean(o * o, axis=-1, keepdims=True) + LN_EPS)
    o_ref[...] = (o * sub_ref[...] * (1.0 - lam_init)).astype(o_ref.dtype)


def _diff_attention(qkv, positions, rel_bias, lam_vecs, subln_w, lam_init, *, tq=256, tk=512):
    B, S, D3 = qkv.shape
    D = D3 // 3
    H = D // (2 * DA_HEAD_DIM)
    w = 2 * DA_HEAD_DIM
    tq = _pick(S, (tq, 128))
    tk = _pick(S, (tk, 256, 128))
    nq, nk = S // tq, S // tk
    thr = _t5_thresholds()

    pq = positions.reshape(B, nq, tq)
    pk = positions.reshape(B, nk, tk)
    qmin, qmax = pq.min(-1), pq.max(-1)
    kmin, kmax = pk.min(-1), pk.max(-1)
    needed = kmin[:, None, :] <= qmax[:, :, None]
    last = jnp.max(jnp.where(needed, jnp.arange(nk, dtype=jnp.int32), 0), axis=-1).astype(jnp.int32)

    grid_spec = pltpu.PrefetchScalarGridSpec(
        num_scalar_prefetch=5,
        grid=(B, H, nq),
        in_specs=[
            pl.BlockSpec((None, tq, w), lambda b, h, i, *_: (b, i, h)),
            pl.BlockSpec((None, S, w), lambda b, h, i, *_: (b, 0, H + h)),
            pl.BlockSpec((None, S, w), lambda b, h, i, *_: (b, 0, 2 * H + h)),
            pl.BlockSpec((None, tq, 1), lambda b, h, i, *_: (b, i, 0)),
            pl.BlockSpec((None, 1, S), lambda b, h, i, *_: (b, 0, 0)),
            pl.BlockSpec(memory_space=pltpu.SMEM),
            pl.BlockSpec((4, DA_HEAD_DIM), lambda b, h, i, *_: (0, 0)),
            pl.BlockSpec((1, w), lambda b, h, i, *_: (0, 0)),
        ],
        out_specs=pl.BlockSpec((None, tq, w), lambda b, h, i, *_: (b, i, h)),
        scratch_shapes=[pltpu.VMEM((2, tq, 1), F32), pltpu.VMEM((2, tq, 1), F32),
                        pltpu.VMEM((2, tq, w), F32)],
    )
    return pl.pallas_call(
        functools.partial(_attn_kernel, tk=tk, thr=thr, lam_init=lam_init),
        out_shape=jax.ShapeDtypeStruct((B, S, D), BF16),
        grid_spec=grid_spec,
        compiler_params=_params(("parallel", "parallel", "arbitrary")),
        name="diff_attention",
    )(qmin, qmax, kmin, kmax, last,
      qkv, qkv, qkv, positions.reshape(B, S, 1), positions.reshape(B, 1, S),
      rel_bias.astype(F32).T, lam_vecs.astype(F32), subln_w.astype(F32).reshape(1, w))


def _xattn_kernel(xb_ref, xf_ref, wq_ref, kv_ref, wo_ref, g_ref, b_ref, o_ref, ob_ref, *, alpha):
    dh = CA_HEAD_DIM
    q = jnp.dot(xb_ref[...], wq_ref[...], preferred_element_type=F32) * (dh ** -0.5)
    q = q.astype(BF16)
    heads = []
    for hh in range(CA_HEADS):
        kh = kv_ref[:, hh * dh:(hh + 1) * dh]
        vh = kv_ref[:, CA_DIM + hh * dh:CA_DIM + (hh + 1) * dh]
        s = lax.dot_general(q[:, hh * dh:(hh + 1) * dh], kh, (((1,), (1,)), ((), ())),
                            preferred_element_type=F32)
        p = jnp.exp(s - jnp.max(s, axis=-1, keepdims=True))
        p = p / jnp.sum(p, axis=-1, keepdims=True)
        heads.append(jnp.dot(p.astype(BF16), vh, preferred_element_type=F32))
    o = jnp.concatenate(heads, axis=-1).astype(BF16)
    hcat = jnp.dot(o, wo_ref[...], preferred_element_type=F32)
    y = _layer_norm(alpha * xf_ref[...] + hcat, g_ref[...], b_ref[...])
    o_ref[...] = y
    ob_ref[...] = y.astype(BF16)


def _cross_attention(xb, xf, kv, w_q, w_o, g, b, alpha, *, B):
    M, D = xf.shape
    S = M // B
    NM = kv.shape[1]
    tm = _pick(S, (256, 128))
    nt = S // tm
    row = pl.BlockSpec((tm, D), lambda i: (i, 0))
    vec = pl.BlockSpec((1, D), lambda i: (0, 0))
    return pl.pallas_call(
        functools.partial(_xattn_kernel, alpha=alpha),
        out_shape=(jax.ShapeDtypeStruct((M, D), F32), jax.ShapeDtypeStruct((M, D), BF16)),
        grid=(M // tm,),
        in_specs=[row, row,
                  pl.BlockSpec((D, CA_DIM), lambda i: (0, 0)),
                  pl.BlockSpec((None, NM, 2 * CA_DIM), lambda i: (i // nt, 0, 0)),
                  pl.BlockSpec((CA_DIM, D), lambda i: (0, 0)),
                  vec, vec],
        out_specs=(row, row),
        compiler_params=_params(("parallel",)),
        name="cross_attention",
    )(xb, xf, w_q, kv, w_o, g.reshape(1, D), b.reshape(1, D))


def _ffn_up_kernel(x_ref, wg_ref, wv_ref, cwg_ref, cwv_ref, cbg_ref, cbv_ref, o_ref,
                   carry_ref, *, tiles_per_seq):
    i = pl.program_id(1)
    tm = x_ref.shape[0]
    x = x_ref[...]
    row = lax.broadcasted_iota(jnp.int32, (tm, 1), 0)
    seq_start = (i % tiles_per_seq) == 0

    def conv(hcur, half, cw_ref, cb_ref):
        prev = jnp.where(seq_start, 0.0, carry_ref[half])
        p1 = prev[7:8, :]
        p2 = prev[6:7, :]
        h1 = jnp.where(row == 0, p1, pltpu.roll(hcur, 1, axis=0))
        h2 = jnp.where(row == 0, p2, jnp.where(row == 1, p1, pltpu.roll(hcur, 2, axis=0)))
        carry_ref[half] = hcur[tm - 8:, :]
        return cw_ref[0:1, :] * h2 + cw_ref[1:2, :] * h1 + cw_ref[2:3, :] * hcur + cb_ref[...]

    gate = conv(jnp.dot(x, wg_ref[...], preferred_element_type=F32), 0, cwg_ref, cbg_ref)
    val = conv(jnp.dot(x, wv_ref[...], preferred_element_type=F32), 1, cwv_ref, cbv_ref)
    o_ref[...] = (gate * jax.nn.sigmoid(gate) * val).astype(o_ref.dtype)


def _ffn_up(xb, w_up, conv_w, conv_b, *, S):
    M, D = xb.shape
    F2 = w_up.shape[1]
    F = F2 // 2
    tn = 256
    assert F % tn == 0
    nf = F // tn
    tm = _pick(S, (1024, 512, 256, 128))
    conv_b = conv_b.reshape(1, F2)
    return pl.pallas_call(
        functools.partial(_ffn_up_kernel, tiles_per_seq=S // tm),
        out_shape=jax.ShapeDtypeStruct((M, F), BF16),
        grid=(nf, M // tm),
        in_specs=[pl.BlockSpec((tm, D), lambda j, i: (i, 0)),
                  pl.BlockSpec((D, tn), lambda j, i: (0, j)),
                  pl.BlockSpec((D, tn), lambda j, i: (0, nf + j)),
                  pl.BlockSpec((CONV_W, tn), lambda j, i: (0, j)),
                  pl.BlockSpec((CONV_W, tn), lambda j, i: (0, nf + j)),
                  pl.BlockSpec((1, tn), lambda j, i: (0, j)),
                  pl.BlockSpec((1, tn), lambda j, i: (0, nf + j))],
        out_specs=pl.BlockSpec((tm, tn), lambda j, i: (i, j)),
        scratch_shapes=[pltpu.VMEM((2, 8, tn), F32)],
        compiler_params=_params(("parallel", "arbitrary")),
        name="ffn_up_conv_glu",
    )(xb, w_up, w_up, conv_w, conv_w, conv_b, conv_b)


def _mix_kernel(x_ref, mix_ref, *rest, tiles_per_seq):
    outs, carry_ref = rest[:-1], rest[-1]
    i = pl.program_id(0)
    tm = x_ref.shape[0]
    x = x_ref[...]
    row = lax.broadcasted_iota(jnp.int32, (tm, 1), 0)
    prev_last = jnp.where((i % tiles_per_seq) == 0, 0.0, carry_ref[7:8, :])
    xx = jnp.where(row == 0, prev_last, pltpu.roll(x, 1, axis=0)) - x
    carry_ref[...] = x[tm - 8:, :]
    for m, o_ref in enumerate(outs):
        o_ref[...] = (x + xx * mix_ref[m:m + 1, :]).astype(BF16)


def _token_shift_mix(x, mix, *, S):
    M, D = x.shape
    tm = _pick(S, (256, 128))
    n = mix.shape[0]
    row = pl.BlockSpec((tm, D), lambda i: (i, 0))
    return pl.pallas_call(
        functools.partial(_mix_kernel, tiles_per_seq=S // tm),
        out_shape=tuple(jax.ShapeDtypeStruct((M, D), BF16) for _ in range(n)),
        grid=(M // tm,),
        in_specs=[row, pl.BlockSpec((n, D), lambda i: (0, 0))],
        out_specs=tuple(row for _ in range(n)),
        scratch_shapes=[pltpu.VMEM((8, D), F32)],
        compiler_params=_params(("arbitrary",)),
        name="token_shift_mix",
    )(x, mix)


def _lora_kernel(x_ref, w1_ref, w2_ref, b_ref, o_ref, *, mode):
    t = jnp.dot(x_ref[...], w1_ref[...], preferred_element_type=F32)
    if mode == "decay":
        t = jnp.tanh(t)
    elif mode == "gate":
        t = jax.nn.sigmoid(t)
    z = jnp.dot(t.astype(BF16), w2_ref[...], preferred_element_type=F32) + b_ref[...]
    if mode == "decay":
        u = -z
        sp = jnp.maximum(u, 0.0) + jnp.log(1.0 + jnp.exp(-jnp.abs(u)))
        z = -jnp.exp(-sp - 0.5)
    elif mode == "lr":
        z = jax.nn.sigmoid(z)
    o_ref[...] = z


def _lora(xb, w1, w2, bias, mode):
    M, D = xb.shape
    R = w1.shape[1]
    tm = _pick(M, (512, 256, 128))
    row = pl.BlockSpec((tm, D), lambda i: (i, 0))
    return pl.pallas_call(
        functools.partial(_lora_kernel, mode=mode),
        out_shape=jax.ShapeDtypeStruct((M, D), F32),
        grid=(M // tm,),
        in_specs=[row, pl.BlockSpec((D, R), lambda i: (0, 0)),
                  pl.BlockSpec((R, D), lambda i: (0, 0)),
                  pl.BlockSpec((1, D), lambda i: (0, 0))],
        out_specs=row,
        compiler_params=_params(("parallel",)),
        name="lora_" + mode,
    )(xb, w1, w2, bias.reshape(1, D))


def _dot_hi(a, b):
    return jnp.dot(a, b, precision=_HI, preferred_element_type=F32)


def _dot_bf(a, b):
    return jnp.dot(a.astype(BF16), b.astype(BF16), preferred_element_type=F32)


def _head_ones():
    r = lax.broadcasted_iota(jnp.int32, (_LANES, _LANES), 0) // RW_HEAD
    c = lax.broadcasted_iota(jnp.int32, (_LANES, _LANES), 1) // RW_HEAD
    return jnp.where(r == c, 1.0, 0.0).astype(F32)


def _rwkv_prep_kernel(r_ref, k_ref, v_ref, lw_ref, a_ref, kk_ref, ka_ref, rk_ref, lb_ref,
                      rp_ref, y0_ref, e_ref, g_ref, h_ref, *, npairs):
    L = RW_CHUNK
    n2 = 2 * L
    ones_bd = _head_ones()
    ri = lax.broadcasted_iota(jnp.int32, (n2, n2), 0)
    ci = lax.broadcasted_iota(jnp.int32, (n2, n2), 1)
    strict = ci < ri
    incl = ci <= ri
    eye = jnp.where(ri == ci, 1.0, 0.0).astype(F32)
    tri = jnp.where(lax.broadcasted_iota(jnp.int32, (L, L), 1)
                    <= lax.broadcasted_iota(jnp.int32, (L, L), 0), 1.0, 0.0).astype(F32)
    head0 = lax.broadcasted_iota(jnp.int32, (L, _LANES), 1) < RW_HEAD

    def stack(x):
        return jnp.concatenate([jnp.where(head0, x, 0.0), jnp.where(head0, 0.0, x)], axis=0)

    def unstack(x):
        return x[:L] + x[L:]

    for p in range(npairs):
        sl = slice(p * _LANES, (p + 1) * _LANES)
        r = r_ref[:, sl]
        k = k_ref[:, sl]
        v = v_ref[:, sl]
        lw = lw_ref[:, sl]
        a = a_ref[:, sl]
        kk = k * kk_ref[:, sl]
        norm = jnp.sqrt(_dot_hi(kk * kk, ones_bd))
        kk = kk / jnp.maximum(norm, 1e-12)
        kmod = k * (1.0 + (a - 1.0) * ka_ref[:, sl])
        av = -kk
        bv = kk * a
        cl = _dot_hi(tri, lw)
        cl_end = cl[L - 1:L, :]
        c_in = jnp.exp(cl)
        c_ex = jnp.exp(cl - lw)
        c_inv = jnp.exp(-cl)
        c_rem = jnp.exp(cl_end - cl)
        a_s = stack(av * c_ex)
        r_s = stack(r * c_in)
        b_s = stack(bv * c_inv)
        k_s = stack(kmod * c_inv)
        v_s = stack(v)
        bk_t = jnp.concatenate([b_s, k_s], axis=0).T
        a_bk = _dot_bf(a_s, bk_t)
        r_bk = _dot_bf(r_s, bk_t)
        nab = jnp.where(strict, a_bk[:, :n2], 0.0)
        nak = jnp.where(strict, a_bk[:, n2:], 0.0)
        mrb = jnp.where(incl, r_bk[:, :n2], 0.0)
        mrk = jnp.where(incl, r_bk[:, n2:], 0.0)
        t_inv = eye + nab
        x = nab
        for _ in range(int(math.log2(L)) - 1):
            x = _dot_bf(x, x)
            t_inv = t_inv + _dot_bf(t_inv, x)
        w1 = jnp.concatenate([a_s, _dot_bf(nak, v_s)], axis=1)
        pq = _dot_bf(t_inv, w1)
        ry = _dot_bf(mrb, pq)
        rp_ref[:, sl] = unstack(r_s + ry[:, :_LANES])
        y0_ref[:, sl] = unstack(ry[:, _LANES:] + _dot_bf(mrk, v_s))
        gh = _dot_bf(stack(bv * c_rem).T, pq)
        kv = _dot_bf(stack(kmod * c_rem).T, v_s)
        g_ref[p] = jnp.where(ri == ci, jnp.exp(cl_end), 0.0) + gh[:, :_LANES]
        h_ref[p] = gh[:, _LANES:] + kv
        bonus = _dot_hi(r * kmod * rk_ref[:, sl], ones_bd)
        e_ref[:, sl] = lb_ref[:, sl] + bonus * v


def _rwkv_scan_kernel(rp_ref, y0_ref, e_ref, gate_ref, g_ref, h_ref, lnw_ref, o_ref, st_ref, *, npairs):
    c = pl.program_id(2)

    @pl.when(c == 0)
    def _():
        st_ref[...] = jnp.zeros(st_ref.shape, F32)

    ones_bd = _head_ones()
    inv_n = 1.0 / RW_HEAD
    for p in range(npairs):
        sl = slice(p * _LANES, (p + 1) * _LANES)
        state = st_ref[p]
        y = _dot_hi(rp_ref[:, sl], state) + y0_ref[:, sl]
        st_ref[p] = _dot_hi(g_ref[p], state) + h_ref[p]
        mu = _dot_hi(y, ones_bd) * inv_n
        yc = y - mu
        var = _dot_hi(yc * yc, ones_bd) * inv_n
        yn = yc * lax.rsqrt(var + GN_EPS)
        o_ref[:, sl] = ((yn * lnw_ref[:, sl] + e_ref[:, sl]) * gate_ref[:, sl]).astype(o_ref.dtype)


def _rwkv_core(r, k, v, lw, a, gate, k_k, k_a, r_k, lnx_w, lnx_b, *, B):
    M, D = r.shape
    S = M // B
    L = RW_CHUNK
    nc = S // L
    ngroups = D // _LANES
    np1 = _pick(ngroups, (4, 2, 1))
    np2 = _pick(ngroups, (8, 4, 2, 1))
    vec = lambda t: t.astype(F32).reshape(1, D)

    w1 = np1 * _LANES
    tile1 = pl.BlockSpec((L, w1), lambda b, c, g: (b * nc + c, g))
    par1 = pl.BlockSpec((1, w1), lambda b, c, g: (0, g))
    mat1 = pl.BlockSpec((None, None, np1, _LANES, _LANES), lambda b, c, g: (b, c, g, 0, 0))
    md = jax.ShapeDtypeStruct((M, D), F32)
    gh = jax.ShapeDtypeStruct((B, nc, ngroups, _LANES, _LANES), F32)
    rp, y0, e, gm, hm = pl.pallas_call(
        functools.partial(_rwkv_prep_kernel, npairs=np1),
        out_shape=(md, md, md, gh, gh),
        grid=(B, nc, ngroups // np1),
        in_specs=[tile1] * 5 + [par1] * 4,
        out_specs=(tile1, tile1, tile1, mat1, mat1),
        compiler_params=_params(("parallel", "parallel", "parallel")),
        name="rwkv_chunk_prep",
    )(r, k, v, lw, a, vec(k_k), vec(k_a), vec(r_k), vec(lnx_b))

    w2 = np2 * _LANES
    tile2 = pl.BlockSpec((L, w2), lambda b, g, c: (b * nc + c, g))
    par2 = pl.BlockSpec((1, w2), lambda b, g, c: (0, g))
    mat2 = pl.BlockSpec((None, None, np2, _LANES, _LANES), lambda b, g, c: (b, c, g, 0, 0))
    return pl.pallas_call(
        functools.partial(_rwkv_scan_kernel, npairs=np2),
        out_shape=jax.ShapeDtypeStruct((M, D), BF16),
        grid=(B, ngroups // np2, nc),
        in_specs=[tile2] * 4 + [mat2, mat2, par2],
        out_specs=tile2,
        scratch_shapes=[pltpu.VMEM((np2, _LANES, _LANES), F32)],
        compiler_params=_params(("parallel", "parallel", "arbitrary")),
        name="rwkv_state_scan",
    )(rp, y0, e, gate, gm, hm, vec(lnx_w))


def kernel(x, mem, positions, rel_bias, da_w_qkv, da_lam, da_subln, da_w_o, rw_mix, rw_w_rkv, rw_w0, rw_w1, rw_w2, rw_a0, rw_a1, rw_a2, rw_g1, rw_g2, rw_k_k, rw_k_a, rw_r_k, rw_lnx_w, rw_lnx_b, rw_w_o, ca_w_q, ca_w_kv, ca_w_o, ffn_w_up, ffn_conv_w, ffn_conv_b, ffn_w_down, ln_g, ln_b):
    B, S, D = x.shape
    M = B * S
    depth = ln_g.shape[0]
    alpha = (2 * depth) ** 0.25
    bf = lambda t: t.astype(BF16)
    memb = bf(mem).reshape(B * mem.shape[1], D)
    F = ffn_w_down.shape[1]
    tk_down = F // 2 if (F // 2) % _LANES == 0 else F

    xf = x.reshape(M, D)
    xb = bf(xf)
    for i in range(depth):
        j = i // 2
        if i % 2 == 0:
            lam_init = 0.8 - 0.6 * math.exp(-0.3 * i)
            qkv = _matmul(xb, bf(da_w_qkv[j]), BF16)
            att = _diff_attention(qkv.reshape(B, S, 3 * D), positions, rel_bias, da_lam[j],
                                  da_subln[j], lam_init)
            h = _matmul(att.reshape(M, D), bf(da_w_o[j]), F32)
        else:
            xr, xw, xk, xv, xa, xg = _token_shift_mix(xf, rw_mix[j].astype(F32), S=S)
            r = _matmul(xr, bf(rw_w_rkv[j, 0]), F32)
            k = _matmul(xk, bf(rw_w_rkv[j, 1]), F32)
            v = _matmul(xv, bf(rw_w_rkv[j, 2]), F32)
            lw = _lora(xw, bf(rw_w1[j]), bf(rw_w2[j]), rw_w0[j].astype(F32), "decay")
            a = _lora(xa, bf(rw_a1[j]), bf(rw_a2[j]), rw_a0[j].astype(F32), "lr")
            g = _lora(xg, bf(rw_g1[j]), bf(rw_g2[j]), jnp.zeros((D,), F32), "gate")
            o = _rwkv_core(r, k, v, lw, a, g, rw_k_k[j], rw_k_a[j], rw_r_k[j].reshape(D),
                           rw_lnx_w[j], rw_lnx_b[j], B=B)
            h = _matmul(o, bf(rw_w_o[j]), F32)
        xf, xb = _deepnorm(xf, h, ln_g[i, 0], ln_b[i, 0], alpha)

        kv = _matmul(memb, bf(ca_w_kv[i]), BF16).reshape(B, mem.shape[1], 2 * CA_DIM)
        xf, xb = _cross_attention(xb, xf, kv, bf(ca_w_q[i]), bf(ca_w_o[i]),
                                  ln_g[i, 1], ln_b[i, 1], alpha, B=B)

        act = _ffn_up(xb, bf(ffn_w_up[i]), ffn_conv_w[i].astype(F32), ffn_conv_b[i].astype(F32), S=S)
        h = _matmul(act, bf(ffn_w_down[i]), F32, tn=512, tk=tk_down)
        xf, xb = _deepnorm(xf, h, ln_g[i, 2], ln_b[i, 2], alpha)
    return xf.reshape(B, S, D)
```

```python
import functools
import math

import jax
import jax.numpy as jnp
from jax import lax
from jax.experimental import pallas as pl
from jax.experimental.pallas import tpu as pltpu

F32 = jnp.float32
BF16 = jnp.bfloat16

_VMEM_LIMIT_BYTES = 56 * 1024 * 1024
_LANES = 128

DA_HEAD_DIM = 128
N_BUCKETS = 32
MAX_DIST = 128
RW_HEAD = 64
GN_EPS = 64e-5
CA_HEADS = 4
CA_HEAD_DIM = 128
CA_DIM = CA_HEADS * CA_HEAD_DIM
CONV_W = 3
LN_EPS = 1e-5
RW_CHUNK = 64

_NEG = float(jnp.finfo(jnp.float32).min)
_LOG2E = math.log2(math.e)


def _params(sem):
    return pltpu.CompilerParams(dimension_semantics=sem, vmem_limit_bytes=_VMEM_LIMIT_BYTES)


def _pick(n, prefs):
    for p in prefs:
        if n % p == 0:
            return p
    return n


def _mm_kernel(x_ref, w_ref, o_ref, *scratch, nk, scaled_tiles, col_scale):
    part = jnp.dot(x_ref[...], w_ref[...], preferred_element_type=F32)

    def finish(acc):
        if scaled_tiles:
            acc = acc * jnp.where(pl.program_id(1) < scaled_tiles, col_scale, 1.0)
        o_ref[...] = acc.astype(o_ref.dtype)

    if nk == 1:
        finish(part)
        return
    (acc_ref,) = scratch
    k = pl.program_id(2)

    @pl.when(k == 0)
    def _():
        acc_ref[...] = part

    @pl.when(jnp.logical_and(k > 0, k < nk - 1))
    def _():
        acc_ref[...] += part

    @pl.when(k == nk - 1)
    def _():
        finish(acc_ref[...] + part)


def _matmul(x, w, out_dtype, *, tm=1024, tn=1024, tk=None, scaled_cols=0, col_scale=1.0):
    M, K = x.shape
    N = w.shape[1]
    tm = _pick(M, (tm, 512, 256, 128))
    tn = _pick(N, (tn, 512, 256, 128))
    tk = K if tk is None else tk
    nk = K // tk
    assert K % tk == 0 and scaled_cols % tn == 0
    scratch = [pltpu.VMEM((tm, tn), F32)] if nk > 1 else []
    return pl.pallas_call(
        functools.partial(_mm_kernel, nk=nk, scaled_tiles=scaled_cols // tn, col_scale=col_scale),
        out_shape=jax.ShapeDtypeStruct((M, N), out_dtype),
        grid=(M // tm, N // tn, nk),
        in_specs=[pl.BlockSpec((tm, tk), lambda i, j, k: (i, k)),
                  pl.BlockSpec((tk, tn), lambda i, j, k: (k, j))],
        out_specs=pl.BlockSpec((tm, tn), lambda i, j, k: (i, j)),
        scratch_shapes=scratch,
        compiler_params=_params(("parallel", "parallel", "arbitrary")),
        name="matmul",
    )(x, w)


def _layer_norm(z, g, b):
    mu = jnp.mean(z, axis=-1, keepdims=True)
    zc = z - mu
    var = jnp.mean(zc * zc, axis=-1, keepdims=True)
    return zc * lax.rsqrt(var + LN_EPS) * g + b


def _ln_kernel(x_ref, h_ref, g_ref, b_ref, o_ref, ob_ref, *, alpha):
    y = _layer_norm(alpha * x_ref[...] + h_ref[...], g_ref[...], b_ref[...])
    o_ref[...] = y
    ob_ref[...] = y.astype(BF16)


def _deepnorm(x, h, g, b, alpha):
    M, D = x.shape
    tm = _pick(M, (256, 128))
    row = pl.BlockSpec((tm, D), lambda i: (i, 0))
    vec = pl.BlockSpec((1, D), lambda i: (0, 0))
    return pl.pallas_call(
        functools.partial(_ln_kernel, alpha=alpha),
        out_shape=(jax.ShapeDtypeStruct((M, D), F32), jax.ShapeDtypeStruct((M, D), BF16)),
        grid=(M // tm,),
        in_specs=[row, row, vec, vec],
        out_specs=(row, row),
        compiler_params=_params(("parallel",)),
        name="deepnorm",
    )(x, h, g.reshape(1, D), b.reshape(1, D))


def _t5_buckets(n_dist):
    max_exact = N_BUCKETS // 2
    out = []
    for n in range(n_dist):
        if n < max_exact:
            out.append(n)
            continue
        val = math.log(n / max_exact) / math.log(MAX_DIST / max_exact) * (N_BUCKETS - max_exact)
        assert n == max_exact or n >= MAX_DIST or abs(val - round(val)) > 1e-3
        out.append(min(max_exact + int(val), N_BUCKETS - 1))
    return out


def _attn_kernel(qmin_ref, qmax_ref, kmin_ref, kmax_ref, last_ref,
                 q_ref, k_ref, v_ref, qp_ref, kp_ref, tbl_ref, lam_ref, sub_ref,
                 o_ref, m_ref, l_ref, acc_ref, *, tk, far_dist, lam_init):
    b = pl.program_id(0)
    i = pl.program_id(2)
    d = DA_HEAD_DIM
    tq = q_ref.shape[0]
    nrep = tk // _LANES
    qmn = qmin_ref[b, i]
    qmx = qmax_ref[b, i]
    far_bias = tbl_ref[:, _LANES - 1:_LANES]

    m_ref[...] = jnp.full(m_ref.shape, _NEG, F32)
    l_ref[...] = jnp.zeros(l_ref.shape, F32)
    acc_ref[...] = jnp.zeros(acc_ref.shape, F32)

    def step(j, near):
        r0 = pl.multiple_of(j * tk, tk)
        kblk = k_ref[pl.ds(r0, tk), :]
        vblk = v_ref[pl.ds(r0, tk), :]
        if near:
            n = qp_ref[...] - kp_ref[:, pl.ds(r0, tk)]
            idx = jnp.clip(n, 0, _LANES - 1)
            table = jnp.broadcast_to(tbl_ref[...], (tq, _LANES))
            bias = jnp.concatenate(
                [jnp.take_along_axis(table, idx[:, c * _LANES:(c + 1) * _LANES], axis=1)
                 for c in range(nrep)], axis=1)
            keep = n >= 0
        for mi in range(2):
            s = lax.dot_general(q_ref[:, mi * d:(mi + 1) * d], kblk[:, mi * d:(mi + 1) * d],
                                (((1,), (1,)), ((), ())), preferred_element_type=F32)
            if near:
                s = jnp.where(keep, s + bias, _NEG)
                shift_bias = 0.0
            else:
                shift_bias = far_bias
            m_prev = m_ref[mi]
            m_new = jnp.maximum(m_prev, jnp.max(s, axis=-1, keepdims=True) + shift_bias)
            alpha = jnp.exp2(m_prev - m_new)
            p = jnp.exp2(s - pltpu.repeat(m_new - shift_bias, nrep, axis=1))
            psum = p[:, :_LANES]
            for c in range(1, nrep):
                psum = psum + p[:, c * _LANES:(c + 1) * _LANES]
            l_ref[mi] = alpha * l_ref[mi] + psum
            acc_ref[mi] = (pltpu.repeat(alpha, 2 * d // _LANES, axis=1) * acc_ref[mi]
                           + jnp.dot(p.astype(BF16), vblk, preferred_element_type=F32))
            m_ref[mi] = m_new

    def body(j, carry):
        active = kmin_ref[b, j] <= qmx
        far = (qmn - kmax_ref[b, j]) >= far_dist

        @pl.when(jnp.logical_and(active, far))
        def _():
            step(j, False)

        @pl.when(jnp.logical_and(active, jnp.logical_not(far)))
        def _():
            step(j, True)

        return carry

    lax.fori_loop(0, last_ref[b, i] + 1, body, 0)

    lv = lam_ref[...]
    lam = (jnp.exp(jnp.sum(lv[0:1] * lv[1:2], axis=-1, keepdims=True))
           - jnp.exp(jnp.sum(lv[2:3] * lv[3:4], axis=-1, keepdims=True)) + lam_init)
    l1 = jnp.sum(l_ref[0], axis=-1, keepdims=True)
    l2 = jnp.sum(l_ref[1], axis=-1, keepdims=True)
    o = acc_ref[0] / l1 - lam * (acc_ref[1] / l2)
    o = o * lax.rsqrt(jnp.mean(o * o, axis=-1, keepdims=True) + LN_EPS)
    o_ref[...] = (o * sub_ref[...] * (1.0 - lam_init)).astype(o_ref.dtype)


def _diff_attention(qkv, positions, rel_bias, lam_vecs, subln_w, lam_init, *, tq=512, tk=512):
    B, S, D3 = qkv.shape
    D = D3 // 3
    H = D // (2 * DA_HEAD_DIM)
    w = 2 * DA_HEAD_DIM
    tq = _pick(S, (tq, 256, 128))
    tk = _pick(S, (tk, 256, 128))
    nq, nk = S // tq, S // tk
    buckets = _t5_buckets(_LANES)
    assert buckets[-1] == N_BUCKETS - 1
    far_dist = buckets.index(N_BUCKETS - 1)
    table = rel_bias.astype(F32)[jnp.array(buckets, jnp.int32), :].T * _LOG2E
    table = table.reshape(H, 1, _LANES)

    pq = positions.reshape(B, nq, tq)
    pk = positions.reshape(B, nk, tk)
    qmin, qmax = pq.min(-1), pq.max(-1)
    kmin, kmax = pk.min(-1), pk.max(-1)
    needed = kmin[:, None, :] <= qmax[:, :, None]
    last = jnp.max(jnp.where(needed, jnp.arange(nk, dtype=jnp.int32), 0), axis=-1).astype(jnp.int32)

    grid_spec = pltpu.PrefetchScalarGridSpec(
        num_scalar_prefetch=5,
        grid=(B, H, nq),
        in_specs=[
            pl.BlockSpec((None, tq, w), lambda b, h, i, *_: (b, i, h)),
            pl.BlockSpec((None, S, w), lambda b, h, i, *_: (b, 0, H + h)),
            pl.BlockSpec((None, S, w), lambda b, h, i, *_: (b, 0, 2 * H + h)),
            pl.BlockSpec((None, tq, 1), lambda b, h, i, *_: (b, i, 0)),
            pl.BlockSpec((None, 1, S), lambda b, h, i, *_: (b, 0, 0)),
            pl.BlockSpec((None, 1, _LANES), lambda b, h, i, *_: (h, 0, 0)),
            pl.BlockSpec((4, DA_HEAD_DIM), lambda b, h, i, *_: (0, 0)),
            pl.BlockSpec((1, w), lambda b, h, i, *_: (0, 0)),
        ],
        out_specs=pl.BlockSpec((None, tq, w), lambda b, h, i, *_: (b, i, h)),
        scratch_shapes=[pltpu.VMEM((2, tq, _LANES), F32), pltpu.VMEM((2, tq, _LANES), F32),
                        pltpu.VMEM((2, tq, w), F32)],
    )
    return pl.pallas_call(
        functools.partial(_attn_kernel, tk=tk, far_dist=far_dist, lam_init=lam_init),
        out_shape=jax.ShapeDtypeStruct((B, S, D), BF16),
        grid_spec=grid_spec,
        compiler_params=_params(("parallel", "parallel", "arbitrary")),
        name="diff_attention",
    )(qmin, qmax, kmin, kmax, last,
      qkv, qkv, qkv, positions.reshape(B, S, 1), positions.reshape(B, 1, S),
      table, lam_vecs.astype(F32), subln_w.astype(F32).reshape(1, w))


def _xattn_kernel(xb_ref, xf_ref, wq_ref, kv_ref, wo_ref, g_ref, b_ref, o_ref, ob_ref, *, alpha):
    dh = CA_HEAD_DIM
    q = jnp.dot(xb_ref[...], wq_ref[...], preferred_element_type=F32) * (dh ** -0.5)
    q = q.astype(BF16)
    heads = []
    for hh in range(CA_HEADS):
        kh = kv_ref[:, hh * dh:(hh + 1) * dh]
        vh = kv_ref[:, CA_DIM + hh * dh:CA_DIM + (hh + 1) * dh]
        s = lax.dot_general(q[:, hh * dh:(hh + 1) * dh], kh, (((1,), (1,)), ((), ())),
                            preferred_element_type=F32)
        p = jnp.exp(s - jnp.max(s, axis=-1, keepdims=True))
        p = p / jnp.sum(p, axis=-1, keepdims=True)
        heads.append(jnp.dot(p.astype(BF16), vh, preferred_element_type=F32))
    o = jnp.concatenate(heads, axis=-1).astype(BF16)
    hcat = jnp.dot(o, wo_ref[...], preferred_element_type=F32)
    y = _layer_norm(alpha * xf_ref[...] + hcat, g_ref[...], b_ref[...])
    o_ref[...] = y
    ob_ref[...] = y.astype(BF16)


def _cross_attention(xb, xf, kv, w_q, w_o, g, b, alpha, *, B):
    M, D = xf.shape
    S = M // B
    NM = kv.shape[1]
    tm = _pick(S, (256, 128))
    nt = S // tm
    row = pl.BlockSpec((tm, D), lambda i: (i, 0))
    vec = pl.BlockSpec((1, D), lambda i: (0, 0))
    return pl.pallas_call(
        functools.partial(_xattn_kernel, alpha=alpha),
        out_shape=(jax.ShapeDtypeStruct((M, D), F32), jax.ShapeDtypeStruct((M, D), BF16)),
        grid=(M // tm,),
        in_specs=[row, row,
                  pl.BlockSpec((D, CA_DIM), lambda i: (0, 0)),
                  pl.BlockSpec((None, NM, 2 * CA_DIM), lambda i: (i // nt, 0, 0)),
                  pl.BlockSpec((CA_DIM, D), lambda i: (0, 0)),
                  vec, vec],
        out_specs=(row, row),
        compiler_params=_params(("parallel",)),
        name="cross_attention",
    )(xb, xf, w_q, kv, w_o, g.reshape(1, D), b.reshape(1, D))


def _ffn_up_kernel(x_ref, wg_ref, wv_ref, cwg_ref, cwv_ref, cbg_ref, cbv_ref, o_ref,
                   carry_ref, *, tiles_per_seq):
    i = pl.program_id(1)
    tm = x_ref.shape[0]
    x = x_ref[...]
    row = lax.broadcasted_iota(jnp.int32, (tm, 1), 0)
    seq_start = (i % tiles_per_seq) == 0

    def conv(hcur, half, cw_ref, cb_ref):
        prev = jnp.where(seq_start, 0.0, carry_ref[half])
        p1 = prev[7:8, :]
        p2 = prev[6:7, :]
        h1 = jnp.where(row == 0, p1, pltpu.roll(hcur, 1, axis=0))
        h2 = jnp.where(row == 0, p2, jnp.where(row == 1, p1, pltpu.roll(hcur, 2, axis=0)))
        carry_ref[half] = hcur[tm - 8:, :]
        return cw_ref[0:1, :] * h2 + cw_ref[1:2, :] * h1 + cw_ref[2:3, :] * hcur + cb_ref[...]

    gate = conv(jnp.dot(x, wg_ref[...], preferred_element_type=F32), 0, cwg_ref, cbg_ref)
    val = conv(jnp.dot(x, wv_ref[...], preferred_element_type=F32), 1, cwv_ref, cbv_ref)
    o_ref[...] = (gate * jax.nn.sigmoid(gate) * val).astype(o_ref.dtype)


def _ffn_up(xb, w_up, conv_w, conv_b, *, S):
    M, D = xb.shape
    F2 = w_up.shape[1]
    F = F2 // 2
    tn = 256
    assert F % tn == 0
    nf = F // tn
    tm = _pick(S, (1024, 512, 256, 128))
    conv_b = conv_b.reshape(1, F2)
    return pl.pallas_call(
        functools.partial(_ffn_up_kernel, tiles_per_seq=S // tm),
        out_shape=jax.ShapeDtypeStruct((M, F), BF16),
        grid=(nf, M // tm),
        in_specs=[pl.BlockSpec((tm, D), lambda j, i: (i, 0)),
                  pl.BlockSpec((D, tn), lambda j, i: (0, j)),
                  pl.BlockSpec((D, tn), lambda j, i: (0, nf + j)),
                  pl.BlockSpec((CONV_W, tn), lambda j, i: (0, j)),
                  pl.BlockSpec((CONV_W, tn), lambda j, i: (0, nf + j)),
                  pl.BlockSpec((1, tn), lambda j, i: (0, j)),
                  pl.BlockSpec((1, tn), lambda j, i: (0, nf + j))],
        out_specs=pl.BlockSpec((tm, tn), lambda j, i: (i, j)),
        scratch_shapes=[pltpu.VMEM((2, 8, tn), F32)],
        compiler_params=_params(("parallel", "arbitrary")),
        name="ffn_up_conv_glu",
    )(xb, w_up, w_up, conv_w, conv_w, conv_b, conv_b)


def _mix_kernel(x_ref, mix_ref, *rest, tiles_per_seq):
    outs, carry_ref = rest[:-1], rest[-1]
    i = pl.program_id(0)
    tm = x_ref.shape[0]
    x = x_ref[...]
    row = lax.broadcasted_iota(jnp.int32, (tm, 1), 0)
    prev_last = jnp.where((i % tiles_per_seq) == 0, 0.0, carry_ref[7:8, :])
    xx = jnp.where(row == 0, prev_last, pltpu.roll(x, 1, axis=0)) - x
    carry_ref[...] = x[tm - 8:, :]
    for m, o_ref in enumerate(outs):
        o_ref[...] = (x + xx * mix_ref[m:m + 1, :]).astype(BF16)


def _token_shift_mix(x, mix, *, S):
    M, D = x.shape
    tm = _pick(S, (256, 128))
    n = mix.shape[0]
    row = pl.BlockSpec((tm, D), lambda i: (i, 0))
    return pl.pallas_call(
        functools.partial(_mix_kernel, tiles_per_seq=S // tm),
        out_shape=tuple(jax.ShapeDtypeStruct((M, D), BF16) for _ in range(n)),
        grid=(M // tm,),
        in_specs=[row, pl.BlockSpec((n, D), lambda i: (0, 0))],
        out_specs=tuple(row for _ in range(n)),
        scratch_shapes=[pltpu.VMEM((8, D), F32)],
        compiler_params=_params(("arbitrary",)),
        name="token_shift_mix",
    )(x, mix)


def _lora_kernel(x_ref, w1_ref, w2_ref, b_ref, o_ref, *, mode):
    t = jnp.dot(x_ref[...], w1_ref[...], preferred_element_type=F32)
    if mode == "decay":
        t = jnp.tanh(t)
    elif mode == "gate":
        t = jax.nn.sigmoid(t)
    z = jnp.dot(t.astype(BF16), w2_ref[...], preferred_element_type=F32) + b_ref[...]
    if mode == "decay":
        u = -z
        sp = jnp.maximum(u, 0.0) + jnp.log(1.0 + jnp.exp(-jnp.abs(u)))
        z = -jnp.exp(-sp - 0.5)
    elif mode == "lr":
        z = jax.nn.sigmoid(z)
    o_ref[...] = z


def _lora(xb, w1, w2, bias, mode):
    M, D = xb.shape
    R = w1.shape[1]
    tm = _pick(M, (512, 256, 128))
    row = pl.BlockSpec((tm, D), lambda i: (i, 0))
    return pl.pallas_call(
        functools.partial(_lora_kernel, mode=mode),
        out_shape=jax.ShapeDtypeStruct((M, D), F32),
        grid=(M // tm,),
        in_specs=[row, pl.BlockSpec((D, R), lambda i: (0, 0)),
                  pl.BlockSpec((R, D), lambda i: (0, 0)),
                  pl.BlockSpec((1, D), lambda i: (0, 0))],
        out_specs=row,
        compiler_params=_params(("parallel",)),
        name="lora_" + mode,
    )(xb, w1, w2, bias.reshape(1, D))


def _dot_bf(a, b):
    return jnp.dot(a.astype(BF16), b.astype(BF16), preferred_element_type=F32)


def _split3(x):
    hi = x.astype(BF16)
    r1 = x - hi.astype(F32)
    mid = r1.astype(BF16)
    lo = (r1 - mid.astype(F32)).astype(BF16)
    return hi, mid, lo


def _dot_exact_rhs(x, w_bf):
    n = x.shape[0]
    r = jnp.dot(jnp.concatenate(_split3(x), axis=0), w_bf, preferred_element_type=F32)
    return r[:n] + r[n:2 * n] + r[2 * n:]


def _dot_exact_lhs(w_bf, x):
    n = x.shape[1]
    r = jnp.dot(w_bf, jnp.concatenate(_split3(x), axis=1), preferred_element_type=F32)
    return r[:, :n] + r[:, n:2 * n] + r[:, 2 * n:]


def _dot_x3(a, b):
    n = a.shape[0]
    a_hi = a.astype(BF16)
    a_lo = (a - a_hi.astype(F32)).astype(BF16)
    b_hi = b.astype(BF16)
    b_lo = (b - b_hi.astype(F32)).astype(BF16)
    r = jnp.dot(jnp.concatenate([a_hi, a_lo], axis=0), b_hi, preferred_element_type=F32)
    return r[:n] + r[n:] + jnp.dot(a_hi, b_lo, preferred_element_type=F32)


def _head_ones():
    r = lax.broadcasted_iota(jnp.int32, (_LANES, _LANES), 0) // RW_HEAD
    c = lax.broadcasted_iota(jnp.int32, (_LANES, _LANES), 1) // RW_HEAD
    return jnp.where(r == c, 1.0, 0.0).astype(BF16)


def _rwkv_prep_kernel(r_ref, k_ref, v_ref, lw_ref, a_ref, kk_ref, ka_ref, rk_ref, lb_ref,
                      rp_ref, y0_ref, e_ref, g_ref, h_ref, *, npairs):
    L = RW_CHUNK
    n2 = 2 * L
    pairs = range(npairs)
    sls = [slice(p * _LANES, (p + 1) * _LANES) for p in pairs]
    ones_bd = _head_ones()
    ri = lax.broadcasted_iota(jnp.int32, (n2, n2), 0)
    ci = lax.broadcasted_iota(jnp.int32, (n2, n2), 1)
    strict = ci < ri
    incl = ci <= ri
    diag = ri == ci
    tri = jnp.where(lax.broadcasted_iota(jnp.int32, (L, L), 1)
                    <= lax.broadcasted_iota(jnp.int32, (L, L), 0), 1.0, 0.0).astype(BF16)
    head0 = lax.broadcasted_iota(jnp.int32, (L, _LANES), 1) < RW_HEAD

    def stack(x):
        return jnp.concatenate([jnp.where(head0, x, 0.0), jnp.where(head0, 0.0, x)], axis=0)

    def unstack(x):
        return x[:L] + x[L:]

    r = [r_ref[:, sl] for sl in sls]
    k = [k_ref[:, sl] for sl in sls]
    v = [v_ref[:, sl] for sl in sls]
    lw = [lw_ref[:, sl] for sl in sls]
    a = [a_ref[:, sl] for sl in sls]
    kk = [k[p] * kk_ref[:, sls[p]] for p in pairs]
    cl = [_dot_exact_lhs(tri, lw[p]) for p in pairs]
    sq = [_dot_exact_rhs(kk[p] * kk[p], ones_bd) for p in pairs]
    kmod = [k[p] * (1.0 + (a[p] - 1.0) * ka_ref[:, sls[p]]) for p in pairs]
    bonus = [_dot_exact_rhs(r[p] * kmod[p] * rk_ref[:, sls[p]], ones_bd) for p in pairs]
    for p in pairs:
        e_ref[:, sls[p]] = lb_ref[:, sls[p]] + bonus[p] * v[p]
    kk = [kk[p] / jnp.maximum(jnp.sqrt(sq[p]), 1e-12) for p in pairs]
    bv = [kk[p] * a[p] for p in pairs]
    cl_end = [c[L - 1:L, :] for c in cl]
    c_inv = [jnp.exp(-c) for c in cl]
    c_rem = [jnp.exp(cl_end[p] - cl[p]) for p in pairs]
    a_s = [stack(-kk[p] * jnp.exp(cl[p] - lw[p])) for p in pairs]
    r_s = [stack(r[p] * jnp.exp(cl[p])) for p in pairs]
    v_s = [stack(v[p]).astype(BF16) for p in pairs]
    bk_t = [jnp.concatenate([stack(bv[p] * c_inv[p]), stack(kmod[p] * c_inv[p])], axis=0)
            .astype(BF16).T for p in pairs]
    a_bk = [_dot_bf(a_s[p], bk_t[p]) for p in pairs]
    r_bk = [_dot_bf(r_s[p], bk_t[p]) for p in pairs]
    nab = [jnp.where(strict, m[:, :n2], 0.0) for m in a_bk]
    nak = [jnp.where(strict, m[:, n2:], 0.0) for m in a_bk]
    mrb = [jnp.where(incl, m[:, :n2], 0.0) for m in r_bk]
    mrk = [jnp.where(incl, m[:, n2:], 0.0) for m in r_bk]
    nak_v = [_dot_bf(nak[p], v_s[p]) for p in pairs]
    mrk_v = [_dot_bf(mrk[p], v_s[p]) for p in pairs]
    kv = [_dot_bf(stack(kmod[p] * c_rem[p]).T, v_s[p]) for p in pairs]
    t_inv = [jnp.where(diag, 1.0, m) for m in nab]
    x = nab
    for _ in range(int(math.log2(L)) - 1):
        x = [_dot_bf(m, m) for m in x]
        t_inv = [t_inv[p] + _dot_bf(t_inv[p], x[p]) for p in pairs]
    pq = [_dot_bf(t_inv[p], jnp.concatenate([a_s[p], nak_v[p]], axis=1)) for p in pairs]
    ry = [_dot_bf(mrb[p], pq[p]) for p in pairs]
    gh = [_dot_bf(stack(bv[p] * c_rem[p]).T, pq[p]) for p in pairs]
    for p in pairs:
        rp_ref[:, sls[p]] = unstack(r_s[p] + ry[p][:, :_LANES])
        y0_ref[:, sls[p]] = unstack(ry[p][:, _LANES:] + mrk_v[p])
        g_ref[p] = jnp.where(diag, jnp.exp(cl_end[p]), 0.0) + gh[p][:, :_LANES]
        h_ref[p] = gh[p][:, _LANES:] + kv[p]


def _rwkv_scan_kernel(rp_ref, y0_ref, e_ref, gate_ref, g_ref, h_ref, lnw_ref, o_ref, st_ref, *, npairs):
    c = pl.program_id(2)

    @pl.when(c == 0)
    def _():
        st_ref[...] = jnp.zeros(st_ref.shape, F32)

    ones_bd = _head_ones()
    inv_n = 1.0 / RW_HEAD
    pairs = range(npairs)
    sls = [slice(p * _LANES, (p + 1) * _LANES) for p in pairs]
    prod = [_dot_x3(jnp.concatenate([g_ref[p], rp_ref[:, sls[p]]], axis=0), st_ref[p]) for p in pairs]
    for p in pairs:
        st_ref[p] = prod[p][:_LANES] + h_ref[p]
    y = [prod[p][_LANES:] + y0_ref[:, sls[p]] for p in pairs]
    mu = [_dot_exact_rhs(y[p], ones_bd) * inv_n for p in pairs]
    yc = [y[p] - mu[p] for p in pairs]
    var = [_dot_exact_rhs(yc[p] * yc[p], ones_bd) * inv_n for p in pairs]
    for p in pairs:
        yn = yc[p] * lax.rsqrt(var[p] + GN_EPS)
        o_ref[:, sls[p]] = ((yn * lnw_ref[:, sls[p]] + e_ref[:, sls[p]])
                            * gate_ref[:, sls[p]]).astype(o_ref.dtype)


def _rwkv_core(r, k, v, lw, a, gate, k_k, k_a, r_k, lnx_w, lnx_b, *, B):
    M, D = r.shape
    S = M // B
    L = RW_CHUNK
    nc = S // L
    ngroups = D // _LANES
    np1 = _pick(ngroups, (4, 2, 1))
    np2 = _pick(ngroups, (8, 4, 2, 1))
    vec = lambda t: t.astype(F32).reshape(1, D)

    w1 = np1 * _LANES
    tile1 = pl.BlockSpec((L, w1), lambda b, c, g: (b * nc + c, g))
    par1 = pl.BlockSpec((1, w1), lambda b, c, g: (0, g))
    mat1 = pl.BlockSpec((None, None, np1, _LANES, _LANES), lambda b, c, g: (b, c, g, 0, 0))
    md = jax.ShapeDtypeStruct((M, D), F32)
    gh = jax.ShapeDtypeStruct((B, nc, ngroups, _LANES, _LANES), F32)
    rp, y0, e, gm, hm = pl.pallas_call(
        functools.partial(_rwkv_prep_kernel, npairs=np1),
        out_shape=(md, md, md, gh, gh),
        grid=(B, nc, ngroups // np1),
        in_specs=[tile1] * 5 + [par1] * 4,
        out_specs=(tile1, tile1, tile1, mat1, mat1),
        compiler_params=_params(("parallel", "parallel", "parallel")),
        name="rwkv_chunk_prep",
    )(r, k, v, lw, a, vec(k_k), vec(k_a), vec(r_k), vec(lnx_b))

    w2 = np2 * _LANES
    tile2 = pl.BlockSpec((L, w2), lambda b, g, c: (b * nc + c, g))
    par2 = pl.BlockSpec((1, w2), lambda b, g, c: (0, g))
    mat2 = pl.BlockSpec((None, None, np2, _LANES, _LANES), lambda b, g, c: (b, c, g, 0, 0))
    return pl.pallas_call(
        functools.partial(_rwkv_scan_kernel, npairs=np2),
        out_shape=jax.ShapeDtypeStruct((M, D), BF16),
        grid=(B, ngroups // np2, nc),
        in_specs=[tile2] * 4 + [mat2, mat2, par2],
        out_specs=tile2,
        scratch_shapes=[pltpu.VMEM((np2, _LANES, _LANES), F32)],
        compiler_params=_params(("parallel", "parallel", "arbitrary")),
        name="rwkv_state_scan",
    )(rp, y0, e, gate, gm, hm, vec(lnx_w))


def kernel(x, mem, positions, rel_bias, da_w_qkv, da_lam, da_subln, da_w_o, rw_mix, rw_w_rkv, rw_w0, rw_w1, rw_w2, rw_a0, rw_a1, rw_a2, rw_g1, rw_g2, rw_k_k, rw_k_a, rw_r_k, rw_lnx_w, rw_lnx_b, rw_w_o, ca_w_q, ca_w_kv, ca_w_o, ffn_w_up, ffn_conv_w, ffn_conv_b, ffn_w_down, ln_g, ln_b):
    B, S, D = x.shape
    M = B * S
    depth = ln_g.shape[0]
    alpha = (2 * depth) ** 0.25
    bf = lambda t: t.astype(BF16)
    memb = bf(mem).reshape(B * mem.shape[1], D)
    F = ffn_w_down.shape[1]
    tk_down = F // 2 if (F // 2) % _LANES == 0 else F

    xf = x.reshape(M, D)
    xb = bf(xf)
    for i in range(depth):
        j = i // 2
        if i % 2 == 0:
            lam_init = 0.8 - 0.6 * math.exp(-0.3 * i)
            qkv = _matmul(xb, bf(da_w_qkv[j]), BF16, scaled_cols=D,
                          col_scale=DA_HEAD_DIM ** -0.5 * _LOG2E)
            att = _diff_attention(qkv.reshape(B, S, 3 * D), positions, rel_bias, da_lam[j],
                                  da_subln[j], lam_init)
            h = _matmul(att.reshape(M, D), bf(da_w_o[j]), F32)
        else:
            xr, xw, xk, xv, xa, xg = _token_shift_mix(xf, rw_mix[j].astype(F32), S=S)
            r = _matmul(xr, bf(rw_w_rkv[j, 0]), F32)
            k = _matmul(xk, bf(rw_w_rkv[j, 1]), F32)
            v = _matmul(xv, bf(rw_w_rkv[j, 2]), F32)
            lw = _lora(xw, bf(rw_w1[j]), bf(rw_w2[j]), rw_w0[j].astype(F32), "decay")
            a = _lora(xa, bf(rw_a1[j]), bf(rw_a2[j]), rw_a0[j].astype(F32), "lr")
            g = _lora(xg, bf(rw_g1[j]), bf(rw_g2[j]), jnp.zeros((D,), F32), "gate")
            o = _rwkv_core(r, k, v, lw, a, g, rw_k_k[j], rw_k_a[j], rw_r_k[j].reshape(D),
                           rw_lnx_w[j], rw_lnx_b[j], B=B)
            h = _matmul(o, bf(rw_w_o[j]), F32)
        xf, xb = _deepnorm(xf, h, ln_g[i, 0], ln_b[i, 0], alpha)

        kv = _matmul(memb, bf(ca_w_kv[i]), BF16).reshape(B, mem.shape[1], 2 * CA_DIM)
        xf, xb = _cross_attention(xb, xf, kv, bf(ca_w_q[i]), bf(ca_w_o[i]),
                                  ln_g[i, 1], ln_b[i, 1], alpha, B=B)

        act = _ffn_up(xb, bf(ffn_w_up[i]), ffn_conv_w[i].astype(F32), ffn_conv_b[i].astype(F32), S=S)
        h = _matmul(act, bf(ffn_w_down[i]), F32, tn=512, tk=tk_down)
        xf, xb = _deepnorm(xf, h, ln_g[i, 2], ln_b[i, 2], alpha)
    return xf.reshape(B, S, D)
```

```python
import functools
import math

import jax
import jax.numpy as jnp
from jax import lax
from jax.experimental import pallas as pl
from jax.experimental.pallas import tpu as pltpu

F32 = jnp.float32
BF16 = jnp.bfloat16

_VMEM_LIMIT_BYTES = 56 * 1024 * 1024
_LANES = 128

DA_HEAD_DIM = 128
N_BUCKETS = 32
MAX_DIST = 128
RW_HEAD = 64
GN_EPS = 64e-5
CA_HEADS = 4
CA_HEAD_DIM = 128
CA_DIM = CA_HEADS * CA_HEAD_DIM
CONV_W = 3
LN_EPS = 1e-5
RW_CHUNK = 64

_NEG = float(jnp.finfo(jnp.float32).min)
_LOG2E = math.log2(math.e)


def _params(sem):
    return pltpu.CompilerParams(dimension_semantics=sem, vmem_limit_bytes=_VMEM_LIMIT_BYTES)


def _pick(n, prefs):
    for p in prefs:
        if n % p == 0:
            return p
    return n


def _mm_kernel(x_ref, w_ref, o_ref, *scratch, nk, scaled_tiles, col_scale):
    part = jnp.dot(x_ref[...], w_ref[...], preferred_element_type=F32)

    def finish(acc):
        if scaled_tiles:
            acc = acc * jnp.where(pl.program_id(1) < scaled_tiles, col_scale, 1.0)
        o_ref[...] = acc.astype(o_ref.dtype)

    if nk == 1:
        finish(part)
        return
    (acc_ref,) = scratch
    k = pl.program_id(2)

    @pl.when(k == 0)
    def _():
        acc_ref[...] = part

    @pl.when(jnp.logical_and(k > 0, k < nk - 1))
    def _():
        acc_ref[...] += part

    @pl.when(k == nk - 1)
    def _():
        finish(acc_ref[...] + part)


def _matmul(x, w, out_dtype, *, tm=1024, tn=1024, tk=None, scaled_cols=0, col_scale=1.0):
    M, K = x.shape
    N = w.shape[1]
    tm = _pick(M, (tm, 512, 256, 128))
    tn = _pick(N, (tn, 512, 256, 128))
    tk = K if tk is None else tk
    nk = K // tk
    assert K % tk == 0 and scaled_cols % tn == 0
    scratch = [pltpu.VMEM((tm, tn), F32)] if nk > 1 else []
    return pl.pallas_call(
        functools.partial(_mm_kernel, nk=nk, scaled_tiles=scaled_cols // tn, col_scale=col_scale),
        out_shape=jax.ShapeDtypeStruct((M, N), out_dtype),
        grid=(M // tm, N // tn, nk),
        in_specs=[pl.BlockSpec((tm, tk), lambda i, j, k: (i, k)),
                  pl.BlockSpec((tk, tn), lambda i, j, k: (k, j))],
        out_specs=pl.BlockSpec((tm, tn), lambda i, j, k: (i, j)),
        scratch_shapes=scratch,
        compiler_params=_params(("parallel", "parallel", "arbitrary")),
        name="matmul",
    )(x, w)


def _mm_wres_kernel(x_ref, w_ref, o_ref, wb_ref, *, scaled_tiles, col_scale):
    @pl.when(pl.program_id(1) == 0)
    def _():
        wb_ref[...] = w_ref[...].astype(BF16)

    acc = jnp.dot(x_ref[...], wb_ref[...], preferred_element_type=F32)
    if scaled_tiles:
        acc = acc * jnp.where(pl.program_id(0) < scaled_tiles, col_scale, 1.0)
    o_ref[...] = acc.astype(o_ref.dtype)


def _matmul_wres(x, w, out_dtype, *, tm=1024, tn=512, scaled_cols=0, col_scale=1.0):
    M, K = x.shape
    N = w.shape[1]
    tm = _pick(M, (tm, 512, 256, 128))
    tn = _pick(N, (tn, 256, 128))
    assert scaled_cols % tn == 0
    return pl.pallas_call(
        functools.partial(_mm_wres_kernel, scaled_tiles=scaled_cols // tn, col_scale=col_scale),
        out_shape=jax.ShapeDtypeStruct((M, N), out_dtype),
        grid=(N // tn, M // tm),
        in_specs=[pl.BlockSpec((tm, K), lambda j, i: (i, 0)),
                  pl.BlockSpec((K, tn), lambda j, i: (0, j))],
        out_specs=pl.BlockSpec((tm, tn), lambda j, i: (i, j)),
        scratch_shapes=[pltpu.VMEM((K, tn), BF16)],
        compiler_params=_params(("parallel", "arbitrary")),
        name="matmul_wres",
    )(x, w.astype(F32))


def _layer_norm(z, g, b):
    mu = jnp.mean(z, axis=-1, keepdims=True)
    zc = z - mu
    var = jnp.mean(zc * zc, axis=-1, keepdims=True)
    return zc * lax.rsqrt(var + LN_EPS) * g + b


def _ln_kernel(x_ref, h_ref, g_ref, b_ref, o_ref, ob_ref, *, alpha):
    y = _layer_norm(alpha * x_ref[...] + h_ref[...], g_ref[...], b_ref[...])
    o_ref[...] = y
    ob_ref[...] = y.astype(BF16)


def _deepnorm(x, h, g, b, alpha):
    M, D = x.shape
    tm = _pick(M, (256, 128))
    row = pl.BlockSpec((tm, D), lambda i: (i, 0))
    vec = pl.BlockSpec((1, D), lambda i: (0, 0))
    return pl.pallas_call(
        functools.partial(_ln_kernel, alpha=alpha),
        out_shape=(jax.ShapeDtypeStruct((M, D), F32), jax.ShapeDtypeStruct((M, D), BF16)),
        grid=(M // tm,),
        in_specs=[row, row, vec, vec],
        out_specs=(row, row),
        compiler_params=_params(("parallel",)),
        name="deepnorm",
    )(x, h, g.reshape(1, D), b.reshape(1, D))


def _t5_buckets(n_dist):
    max_exact = N_BUCKETS // 2
    out = []
    for n in range(n_dist):
        if n < max_exact:
            out.append(n)
            continue
        val = math.log(n / max_exact) / math.log(MAX_DIST / max_exact) * (N_BUCKETS - max_exact)
        assert n == max_exact or n >= MAX_DIST or abs(val - round(val)) > 1e-3
        out.append(min(max_exact + int(val), N_BUCKETS - 1))
    return out


def _attn_kernel(qmin_ref, qmax_ref, kmin_ref, kmax_ref, last_ref, nfar_ref,
                 q_ref, k_ref, v_ref, qp_ref, kp_ref, tbl_ref, lam_ref, sub_ref,
                 o_ref, m_ref, l_ref, acc_ref, *, tk, far_dist, lam_init):
    b = pl.program_id(0)
    i = pl.program_id(2)
    d = DA_HEAD_DIM
    tq = q_ref.shape[0]
    nrep = tk // _LANES
    qmn = qmin_ref[b, i]
    qmx = qmax_ref[b, i]
    far_bias = tbl_ref[:, _LANES - 1:_LANES]

    m_ref[...] = jnp.full(m_ref.shape, _NEG, F32)
    l_ref[...] = jnp.zeros(l_ref.shape, F32)
    acc_ref[...] = jnp.zeros(acc_ref.shape, F32)

    def scores(j):
        kblk = k_ref[pl.ds(pl.multiple_of(j * tk, tk), tk), :]
        return [lax.dot_general(q_ref[:, mi * d:(mi + 1) * d], kblk[:, mi * d:(mi + 1) * d],
                                (((1,), (1,)), ((), ())), preferred_element_type=F32)
                for mi in range(2)]

    def update(j, s, near):
        r0 = pl.multiple_of(j * tk, tk)
        vblk = v_ref[pl.ds(r0, tk), :]
        if near:
            n = qp_ref[...] - kp_ref[:, pl.ds(r0, tk)]
            idx = jnp.clip(n, 0, _LANES - 1)
            table = jnp.broadcast_to(tbl_ref[...], (tq, _LANES))
            bias = jnp.concatenate(
                [jnp.take_along_axis(table, idx[:, c * _LANES:(c + 1) * _LANES], axis=1)
                 for c in range(nrep)], axis=1)
            keep = n >= 0
            s = [jnp.where(keep, sm + bias, _NEG) for sm in s]
            shift_bias = 0.0
        else:
            shift_bias = far_bias
        maps = range(2)
        m_prev = [m_ref[mi] for mi in maps]
        m_new = [jnp.maximum(m_prev[mi], jnp.max(s[mi], axis=-1, keepdims=True) + shift_bias)
                 for mi in maps]
        alpha = [jnp.exp2(m_prev[mi] - m_new[mi]) for mi in maps]
        p = [jnp.exp2(s[mi] - jnp.tile(m_new[mi] - shift_bias, (1, nrep))) for mi in maps]
        for mi in maps:
            psum = p[mi][:, :_LANES]
            for c in range(1, nrep):
                psum = psum + p[mi][:, c * _LANES:(c + 1) * _LANES]
            l_ref[mi] = alpha[mi] * l_ref[mi] + psum
            m_ref[mi] = m_new[mi]
        pv = [jnp.dot(p[mi].astype(BF16), vblk, preferred_element_type=F32) for mi in maps]
        for mi in maps:
            acc_ref[mi] = jnp.tile(alpha[mi], (1, 2 * d // _LANES)) * acc_ref[mi] + pv[mi]

    def pair_body(t, carry):
        s_a = scores(2 * t)
        s_b = scores(2 * t + 1)
        update(2 * t, s_a, False)
        update(2 * t + 1, s_b, False)
        return carry

    npair = nfar_ref[b, i] // 2
    lax.fori_loop(0, npair, pair_body, 0)

    def single_body(j, carry):
        active = kmin_ref[b, j] <= qmx
        far = (qmn - kmax_ref[b, j]) >= far_dist

        @pl.when(jnp.logical_and(active, far))
        def _():
            update(j, scores(j), False)

        @pl.when(jnp.logical_and(active, jnp.logical_not(far)))
        def _():
            update(j, scores(j), True)

        return carry

    lax.fori_loop(2 * npair, last_ref[b, i] + 1, single_body, 0)

    lv = lam_ref[...]
    lam = (jnp.exp(jnp.sum(lv[0:1] * lv[1:2], axis=-1, keepdims=True))
           - jnp.exp(jnp.sum(lv[2:3] * lv[3:4], axis=-1, keepdims=True)) + lam_init)
    l1 = jnp.sum(l_ref[0], axis=-1, keepdims=True)
    l2 = jnp.sum(l_ref[1], axis=-1, keepdims=True)
    o = acc_ref[0] / l1 - lam * (acc_ref[1] / l2)
    o = o * lax.rsqrt(jnp.mean(o * o, axis=-1, keepdims=True) + LN_EPS)
    o_ref[...] = (o * sub_ref[...] * (1.0 - lam_init)).astype(o_ref.dtype)


def _diff_attention(qkv, positions, rel_bias, lam_vecs, subln_w, lam_init, *, tq=512, tk=512):
    B, S, D3 = qkv.shape
    D = D3 // 3
    H = D // (2 * DA_HEAD_DIM)
    w = 2 * DA_HEAD_DIM
    tq = _pick(S, (tq, 256, 128))
    tk = _pick(S, (tk, 256, 128))
    nq, nk = S // tq, S // tk
    buckets = _t5_buckets(_LANES)
    assert buckets[-1] == N_BUCKETS - 1
    far_dist = buckets.index(N_BUCKETS - 1)
    table = rel_bias.astype(F32)[jnp.array(buckets, jnp.int32), :].T * _LOG2E
    table = table.reshape(H, 1, _LANES)

    pq = positions.reshape(B, nq, tq)
    pk = positions.reshape(B, nk, tk)
    qmin, qmax = pq.min(-1), pq.max(-1)
    kmin, kmax = pk.min(-1), pk.max(-1)
    needed = kmin[:, None, :] <= qmax[:, :, None]
    last = jnp.max(jnp.where(needed, jnp.arange(nk, dtype=jnp.int32), 0), axis=-1).astype(jnp.int32)
    far = (qmin[:, :, None] - kmax[:, None, :]) >= far_dist
    nfar = jnp.sum(jnp.cumprod(far.astype(jnp.int32), axis=-1), axis=-1).astype(jnp.int32)
    nfar = jnp.minimum(nfar, last + 1)

    grid_spec = pltpu.PrefetchScalarGridSpec(
        num_scalar_prefetch=6,
        grid=(B, H, nq),
        in_specs=[
            pl.BlockSpec((None, tq, w), lambda b, h, i, *_: (b, i, h)),
            pl.BlockSpec((None, S, w), lambda b, h, i, *_: (b, 0, H + h)),
            pl.BlockSpec((None, S, w), lambda b, h, i, *_: (b, 0, 2 * H + h)),
            pl.BlockSpec((None, tq, 1), lambda b, h, i, *_: (b, i, 0)),
            pl.BlockSpec((None, 1, S), lambda b, h, i, *_: (b, 0, 0)),
            pl.BlockSpec((None, 1, _LANES), lambda b, h, i, *_: (h, 0, 0)),
            pl.BlockSpec((4, DA_HEAD_DIM), lambda b, h, i, *_: (0, 0)),
            pl.BlockSpec((1, w), lambda b, h, i, *_: (0, 0)),
        ],
        out_specs=pl.BlockSpec((None, tq, w), lambda b, h, i, *_: (b, i, h)),
        scratch_shapes=[pltpu.VMEM((2, tq, _LANES), F32), pltpu.VMEM((2, tq, _LANES), F32),
                        pltpu.VMEM((2, tq, w), F32)],
    )
    return pl.pallas_call(
        functools.partial(_attn_kernel, tk=tk, far_dist=far_dist, lam_init=lam_init),
        out_shape=jax.ShapeDtypeStruct((B, S, D), BF16),
        grid_spec=grid_spec,
        compiler_params=_params(("parallel", "parallel", "arbitrary")),
        name="diff_attention",
    )(qmin, qmax, kmin, kmax, last, nfar,
      qkv, qkv, qkv, positions.reshape(B, S, 1), positions.reshape(B, 1, S),
      table, lam_vecs.astype(F32), subln_w.astype(F32).reshape(1, w))


def _xattn_kernel(xb_ref, xf_ref, wq_ref, kv_ref, wo_ref, g_ref, b_ref, o_ref, ob_ref, *, alpha):
    dh = CA_HEAD_DIM
    q = jnp.dot(xb_ref[...], wq_ref[...], preferred_element_type=F32) * (dh ** -0.5)
    q = q.astype(BF16)
    heads = []
    for hh in range(CA_HEADS):
        kh = kv_ref[:, hh * dh:(hh + 1) * dh]
        vh = kv_ref[:, CA_DIM + hh * dh:CA_DIM + (hh + 1) * dh]
        s = lax.dot_general(q[:, hh * dh:(hh + 1) * dh], kh, (((1,), (1,)), ((), ())),
                            preferred_element_type=F32)
        p = jnp.exp(s - jnp.max(s, axis=-1, keepdims=True))
        p = p / jnp.sum(p, axis=-1, keepdims=True)
        heads.append(jnp.dot(p.astype(BF16), vh, preferred_element_type=F32))
    o = jnp.concatenate(heads, axis=-1).astype(BF16)
    hcat = jnp.dot(o, wo_ref[...], preferred_element_type=F32)
    y = _layer_norm(alpha * xf_ref[...] + hcat, g_ref[...], b_ref[...])
    o_ref[...] = y
    ob_ref[...] = y.astype(BF16)


def _cross_attention(xb, xf, kv, w_q, w_o, g, b, alpha, *, B):
    M, D = xf.shape
    S = M // B
    NM = kv.shape[1]
    tm = _pick(S, (256, 128))
    nt = S // tm
    row = pl.BlockSpec((tm, D), lambda i: (i, 0))
    vec = pl.BlockSpec((1, D), lambda i: (0, 0))
    return pl.pallas_call(
        functools.partial(_xattn_kernel, alpha=alpha),
        out_shape=(jax.ShapeDtypeStruct((M, D), F32), jax.ShapeDtypeStruct((M, D), BF16)),
        grid=(M // tm,),
        in_specs=[row, row,
                  pl.BlockSpec((D, CA_DIM), lambda i: (0, 0)),
                  pl.BlockSpec((None, NM, 2 * CA_DIM), lambda i: (i // nt, 0, 0)),
                  pl.BlockSpec((CA_DIM, D), lambda i: (0, 0)),
                  vec, vec],
        out_specs=(row, row),
        compiler_params=_params(("parallel",)),
        name="cross_attention",
    )(xb, xf, w_q, kv, w_o, g.reshape(1, D), b.reshape(1, D))


def _ffn_up_kernel(x_ref, wg_ref, wv_ref, cwg_ref, cwv_ref, cbg_ref, cbv_ref, o_ref,
                   carry_ref, wgb_ref, wvb_ref, *, tiles_per_seq):
    i = pl.program_id(1)
    tm = x_ref.shape[0]
    x = x_ref[...]

    @pl.when(i == 0)
    def _():
        wgb_ref[...] = wg_ref[...].astype(BF16)
        wvb_ref[...] = wv_ref[...].astype(BF16)

    row = lax.broadcasted_iota(jnp.int32, (tm, 1), 0)
    seq_start = (i % tiles_per_seq) == 0

    def conv(hcur, half, cw_ref, cb_ref):
        prev = jnp.where(seq_start, 0.0, carry_ref[half])
        p1 = prev[7:8, :]
        p2 = prev[6:7, :]
        h1 = jnp.where(row == 0, p1, pltpu.roll(hcur, 1, axis=0))
        h2 = jnp.where(row == 0, p2, jnp.where(row == 1, p1, pltpu.roll(hcur, 2, axis=0)))
        carry_ref[half] = hcur[tm - 8:, :]
        return cw_ref[0:1, :] * h2 + cw_ref[1:2, :] * h1 + cw_ref[2:3, :] * hcur + cb_ref[...]

    gate = conv(jnp.dot(x, wgb_ref[...], preferred_element_type=F32), 0, cwg_ref, cbg_ref)
    val = conv(jnp.dot(x, wvb_ref[...], preferred_element_type=F32), 1, cwv_ref, cbv_ref)
    o_ref[...] = (gate * jax.nn.sigmoid(gate) * val).astype(o_ref.dtype)


def _ffn_up(xb, w_up, conv_w, conv_b, *, S):
    M, D = xb.shape
    F2 = w_up.shape[1]
    F = F2 // 2
    tn = 256
    assert F % tn == 0
    nf = F // tn
    tm = _pick(S, (1024, 512, 256, 128))
    conv_b = conv_b.reshape(1, F2)
    return pl.pallas_call(
        functools.partial(_ffn_up_kernel, tiles_per_seq=S // tm),
        out_shape=jax.ShapeDtypeStruct((M, F), BF16),
        grid=(nf, M // tm),
        in_specs=[pl.BlockSpec((tm, D), lambda j, i: (i, 0)),
                  pl.BlockSpec((D, tn), lambda j, i: (0, j)),
                  pl.BlockSpec((D, tn), lambda j, i: (0, nf + j)),
                  pl.BlockSpec((CONV_W, tn), lambda j, i: (0, j)),
                  pl.BlockSpec((CONV_W, tn), lambda j, i: (0, nf + j)),
                  pl.BlockSpec((1, tn), lambda j, i: (0, j)),
                  pl.BlockSpec((1, tn), lambda j, i: (0, nf + j))],
        out_specs=pl.BlockSpec((tm, tn), lambda j, i: (i, j)),
        scratch_shapes=[pltpu.VMEM((2, 8, tn), F32), pltpu.VMEM((D, tn), BF16),
                        pltpu.VMEM((D, tn), BF16)],
        compiler_params=_params(("parallel", "arbitrary")),
        name="ffn_up_conv_glu",
    )(xb, w_up, w_up, conv_w, conv_w, conv_b, conv_b)


def _mix_kernel(x_ref, mix_ref, *rest, tiles_per_seq):
    outs, carry_ref = rest[:-1], rest[-1]
    i = pl.program_id(0)
    tm = x_ref.shape[0]
    x = x_ref[...]
    row = lax.broadcasted_iota(jnp.int32, (tm, 1), 0)
    prev_last = jnp.where((i % tiles_per_seq) == 0, 0.0, carry_ref[7:8, :])
    xx = jnp.where(row == 0, prev_last, pltpu.roll(x, 1, axis=0)) - x
    carry_ref[...] = x[tm - 8:, :]
    for m, o_ref in enumerate(outs):
        o_ref[...] = (x + xx * mix_ref[m:m + 1, :]).astype(BF16)


def _token_shift_mix(x, mix, *, S):
    M, D = x.shape
    tm = _pick(S, (256, 128))
    n = mix.shape[0]
    row = pl.BlockSpec((tm, D), lambda i: (i, 0))
    return pl.pallas_call(
        functools.partial(_mix_kernel, tiles_per_seq=S // tm),
        out_shape=tuple(jax.ShapeDtypeStruct((M, D), BF16) for _ in range(n)),
        grid=(M // tm,),
        in_specs=[row, pl.BlockSpec((n, D), lambda i: (0, 0))],
        out_specs=tuple(row for _ in range(n)),
        scratch_shapes=[pltpu.VMEM((8, D), F32)],
        compiler_params=_params(("arbitrary",)),
        name="token_shift_mix",
    )(x, mix)


def _lora_kernel(x_ref, w1_ref, w2_ref, b_ref, o_ref, *, mode):
    t = jnp.dot(x_ref[...], w1_ref[...], preferred_element_type=F32)
    if mode == "decay":
        t = jnp.tanh(t)
    elif mode == "gate":
        t = jax.nn.sigmoid(t)
    z = jnp.dot(t.astype(BF16), w2_ref[...], preferred_element_type=F32) + b_ref[...]
    if mode == "decay":
        u = -z
        sp = jnp.maximum(u, 0.0) + jnp.log(1.0 + jnp.exp(-jnp.abs(u)))
        z = -jnp.exp(-sp - 0.5)
    elif mode == "lr":
        z = jax.nn.sigmoid(z)
    o_ref[...] = z


def _lora(xb, w1, w2, bias, mode):
    M, D = xb.shape
    R = w1.shape[1]
    tm = _pick(M, (512, 256, 128))
    row = pl.BlockSpec((tm, D), lambda i: (i, 0))
    return pl.pallas_call(
        functools.partial(_lora_kernel, mode=mode),
        out_shape=jax.ShapeDtypeStruct((M, D), F32),
        grid=(M // tm,),
        in_specs=[row, pl.BlockSpec((D, R), lambda i: (0, 0)),
                  pl.BlockSpec((R, D), lambda i: (0, 0)),
                  pl.BlockSpec((1, D), lambda i: (0, 0))],
        out_specs=row,
        compiler_params=_params(("parallel",)),
        name="lora_" + mode,
    )(xb, w1, w2, bias.reshape(1, D))


def _dot_bf(a, b):
    return jnp.dot(a.astype(BF16), b.astype(BF16), preferred_element_type=F32)


def _split3(x):
    hi = x.astype(BF16)
    r1 = x - hi.astype(F32)
    mid = r1.astype(BF16)
    lo = (r1 - mid.astype(F32)).astype(BF16)
    return hi, mid, lo


def _dot_exact_rhs(x, w_bf):
    n = x.shape[0]
    r = jnp.dot(jnp.concatenate(_split3(x), axis=0), w_bf, preferred_element_type=F32)
    return r[:n] + r[n:2 * n] + r[2 * n:]


def _dot_exact_lhs(w_bf, x):
    n = x.shape[1]
    r = jnp.dot(w_bf, jnp.concatenate(_split3(x), axis=1), preferred_element_type=F32)
    return r[:, :n] + r[:, n:2 * n] + r[:, 2 * n:]


def _dot_x3(a, b):
    n = a.shape[0]
    a_hi = a.astype(BF16)
    a_lo = (a - a_hi.astype(F32)).astype(BF16)
    b_hi = b.astype(BF16)
    b_lo = (b - b_hi.astype(F32)).astype(BF16)
    r = jnp.dot(jnp.concatenate([a_hi, a_lo], axis=0), b_hi, preferred_element_type=F32)
    return r[:n] + r[n:] + jnp.dot(a_hi, b_lo, preferred_element_type=F32)


def _head_ones():
    r = lax.broadcasted_iota(jnp.int32, (_LANES, _LANES), 0) // RW_HEAD
    c = lax.broadcasted_iota(jnp.int32, (_LANES, _LANES), 1) // RW_HEAD
    return jnp.where(r == c, 1.0, 0.0).astype(BF16)


def _rwkv_prep_kernel(r_ref, k_ref, v_ref, lw_ref, a_ref, kk_ref, ka_ref, rk_ref, lb_ref,
                      rp_ref, y0_ref, e_ref, g_ref, h_ref, *, npairs):
    L = RW_CHUNK
    n2 = 2 * L
    pairs = range(npairs)
    sls = [slice(p * _LANES, (p + 1) * _LANES) for p in pairs]
    ones_bd = _head_ones()
    ri = lax.broadcasted_iota(jnp.int32, (n2, n2), 0)
    ci = lax.broadcasted_iota(jnp.int32, (n2, n2), 1)
    strict = ci < ri
    incl = ci <= ri
    diag = ri == ci
    tri = jnp.where(lax.broadcasted_iota(jnp.int32, (L, L), 1)
                    <= lax.broadcasted_iota(jnp.int32, (L, L), 0), 1.0, 0.0).astype(BF16)
    head0 = lax.broadcasted_iota(jnp.int32, (L, _LANES), 1) < RW_HEAD

    def stack(x):
        return jnp.concatenate([jnp.where(head0, x, 0.0), jnp.where(head0, 0.0, x)], axis=0)

    def unstack(x):
        return x[:L] + x[L:]

    r = [r_ref[:, sl] for sl in sls]
    k = [k_ref[:, sl] for sl in sls]
    v = [v_ref[:, sl] for sl in sls]
    lw = [lw_ref[:, sl] for sl in sls]
    a = [a_ref[:, sl] for sl in sls]
    kk = [k[p] * kk_ref[:, sls[p]] for p in pairs]
    kmod = [k[p] * (1.0 + (a[p] - 1.0) * ka_ref[:, sls[p]]) for p in pairs]
    cl = [_dot_exact_lhs(tri, lw[p]) for p in pairs]
    hs = [_dot_exact_rhs(jnp.concatenate([kk[p] * kk[p], r[p] * kmod[p] * rk_ref[:, sls[p]]], axis=0),
                         ones_bd) for p in pairs]
    for p in pairs:
        e_ref[:, sls[p]] = lb_ref[:, sls[p]] + hs[p][L:] * v[p]
    kk = [kk[p] / jnp.maximum(jnp.sqrt(hs[p][:L]), 1e-12) for p in pairs]
    bv = [kk[p] * a[p] for p in pairs]
    cl_end = [c[L - 1:L, :] for c in cl]
    c_inv = [jnp.exp(-c) for c in cl]
    c_rem = [jnp.exp(cl_end[p] - cl[p]) for p in pairs]
    a_s = [stack(-kk[p] * jnp.exp(cl[p] - lw[p])) for p in pairs]
    r_s = [stack(r[p] * jnp.exp(cl[p])) for p in pairs]
    v_s = [stack(v[p]).astype(BF16) for p in pairs]
    bk_t = [jnp.concatenate([stack(bv[p] * c_inv[p]), stack(kmod[p] * c_inv[p])], axis=0)
            .astype(BF16).T for p in pairs]
    ar_bk = [_dot_bf(jnp.concatenate([a_s[p], r_s[p]], axis=0), bk_t[p]) for p in pairs]
    nab = [jnp.where(strict, m[:n2, :n2], 0.0) for m in ar_bk]
    nak = [jnp.where(strict, m[:n2, n2:], 0.0) for m in ar_bk]
    mrb = [jnp.where(incl, m[n2:, :n2], 0.0) for m in ar_bk]
    mrk = [jnp.where(incl, m[n2:, n2:], 0.0) for m in ar_bk]
    xv = [_dot_bf(jnp.concatenate([nak[p], mrk[p], stack(kmod[p] * c_rem[p]).T], axis=0), v_s[p])
          for p in pairs]
    levels = int(math.log2(L)) - 1
    t_inv = [jnp.where(diag, 1.0, m) for m in nab]
    x = [_dot_bf(m, m) for m in nab]
    for lvl in range(levels):
        if lvl < levels - 1:
            tx = [_dot_bf(jnp.concatenate([t_inv[p], x[p]], axis=0), x[p]) for p in pairs]
            t_inv = [t_inv[p] + tx[p][:n2] for p in pairs]
            x = [tx[p][n2:] for p in pairs]
        else:
            t_inv = [t_inv[p] + _dot_bf(t_inv[p], x[p]) for p in pairs]
    pq = [_dot_bf(t_inv[p], jnp.concatenate([a_s[p], xv[p][:n2]], axis=1)) for p in pairs]
    rg = [_dot_bf(jnp.concatenate([mrb[p], stack(bv[p] * c_rem[p]).T], axis=0), pq[p]) for p in pairs]
    for p in pairs:
        rp_ref[:, sls[p]] = unstack(r_s[p] + rg[p][:n2, :_LANES])
        y0_ref[:, sls[p]] = unstack(rg[p][:n2, _LANES:] + xv[p][n2:2 * n2])
        g_ref[p] = jnp.where(diag, jnp.exp(cl_end[p]), 0.0) + rg[p][n2:, :_LANES]
        h_ref[p] = rg[p][n2:, _LANES:] + xv[p][2 * n2:]


def _rwkv_scan_kernel(rp_ref, y0_ref, e_ref, gate_ref, g_ref, h_ref, lnw_ref, o_ref, st_ref, *, npairs):
    c = pl.program_id(2)

    @pl.when(c == 0)
    def _():
        st_ref[...] = jnp.zeros(st_ref.shape, F32)

    ones_bd = _head_ones()
    inv_n = 1.0 / RW_HEAD
    pairs = range(npairs)
    sls = [slice(p * _LANES, (p + 1) * _LANES) for p in pairs]
    prod = [_dot_x3(jnp.concatenate([g_ref[p], rp_ref[:, sls[p]]], axis=0), st_ref[p]) for p in pairs]
    for p in pairs:
        st_ref[p] = prod[p][:_LANES] + h_ref[p]
    y = [prod[p][_LANES:] + y0_ref[:, sls[p]] for p in pairs]
    mu = [_dot_exact_rhs(y[p], ones_bd) * inv_n for p in pairs]
    yc = [y[p] - mu[p] for p in pairs]
    var = [_dot_exact_rhs(yc[p] * yc[p], ones_bd) * inv_n for p in pairs]
    for p in pairs:
        yn = yc[p] * lax.rsqrt(var[p] + GN_EPS)
        o_ref[:, sls[p]] = ((yn * lnw_ref[:, sls[p]] + e_ref[:, sls[p]])
                            * gate_ref[:, sls[p]]).astype(o_ref.dtype)


def _rwkv_core(r, k, v, lw, a, gate, k_k, k_a, r_k, lnx_w, lnx_b, *, B):
    M, D = r.shape
    S = M // B
    L = RW_CHUNK
    nc = S // L
    ngroups = D // _LANES
    np1 = _pick(ngroups, (8, 4, 2, 1))
    np2 = _pick(ngroups, (8, 4, 2, 1))
    vec = lambda t: t.astype(F32).reshape(1, D)

    w1 = np1 * _LANES
    tile1 = pl.BlockSpec((L, w1), lambda b, c, g: (b * nc + c, g))
    par1 = pl.BlockSpec((1, w1), lambda b, c, g: (0, g))
    mat1 = pl.BlockSpec((None, None, np1, _LANES, _LANES), lambda b, c, g: (b, c, g, 0, 0))
    md = jax.ShapeDtypeStruct((M, D), F32)
    gh = jax.ShapeDtypeStruct((B, nc, ngroups, _LANES, _LANES), F32)
    rp, y0, e, gm, hm = pl.pallas_call(
        functools.partial(_rwkv_prep_kernel, npairs=np1),
        out_shape=(md, md, md, gh, gh),
        grid=(B, nc, ngroups // np1),
        in_specs=[tile1] * 5 + [par1] * 4,
        out_specs=(tile1, tile1, tile1, mat1, mat1),
        compiler_params=_params(("parallel", "parallel", "parallel")),
        name="rwkv_chunk_prep",
    )(r, k, v, lw, a, vec(k_k), vec(k_a), vec(r_k), vec(lnx_b))

    w2 = np2 * _LANES
    tile2 = pl.BlockSpec((L, w2), lambda b, g, c: (b * nc + c, g))
    par2 = pl.BlockSpec((1, w2), lambda b, g, c: (0, g))
    mat2 = pl.BlockSpec((None, None, np2, _LANES, _LANES), lambda b, g, c: (b, c, g, 0, 0))
    return pl.pallas_call(
        functools.partial(_rwkv_scan_kernel, npairs=np2),
        out_shape=jax.ShapeDtypeStruct((M, D), BF16),
        grid=(B, ngroups // np2, nc),
        in_specs=[tile2] * 4 + [mat2, mat2, par2],
        out_specs=tile2,
        scratch_shapes=[pltpu.VMEM((np2, _LANES, _LANES), F32)],
        compiler_params=_params(("parallel", "parallel", "arbitrary")),
        name="rwkv_state_scan",
    )(rp, y0, e, gate, gm, hm, vec(lnx_w))


def kernel(x, mem, positions, rel_bias, da_w_qkv, da_lam, da_subln, da_w_o, rw_mix, rw_w_rkv, rw_w0, rw_w1, rw_w2, rw_a0, rw_a1, rw_a2, rw_g1, rw_g2, rw_k_k, rw_k_a, rw_r_k, rw_lnx_w, rw_lnx_b, rw_w_o, ca_w_q, ca_w_kv, ca_w_o, ffn_w_up, ffn_conv_w, ffn_conv_b, ffn_w_down, ln_g, ln_b):
    B, S, D = x.shape
    M = B * S
    depth = ln_g.shape[0]
    alpha = (2 * depth) ** 0.25
    bf = lambda t: t.astype(BF16)
    memb = bf(mem).reshape(B * mem.shape[1], D)
    F = ffn_w_down.shape[1]
    tk_down = F // 2 if (F // 2) % _LANES == 0 else F

    xf = x.reshape(M, D)
    xb = bf(xf)
    for i in range(depth):
        j = i // 2
        if i % 2 == 0:
            lam_init = 0.8 - 0.6 * math.exp(-0.3 * i)
            qkv = _matmul_wres(xb, da_w_qkv[j], BF16, scaled_cols=D,
                               col_scale=DA_HEAD_DIM ** -0.5 * _LOG2E)
            att = _diff_attention(qkv.reshape(B, S, 3 * D), positions, rel_bias, da_lam[j],
                                  da_subln[j], lam_init)
            h = _matmul_wres(att.reshape(M, D), da_w_o[j], F32)
        else:
            xr, xw, xk, xv, xa, xg = _token_shift_mix(xf, rw_mix[j].astype(F32), S=S)
            r = _matmul_wres(xr, rw_w_rkv[j, 0], F32)
            k = _matmul_wres(xk, rw_w_rkv[j, 1], F32)
            v = _matmul_wres(xv, rw_w_rkv[j, 2], F32)
            lw = _lora(xw, bf(rw_w1[j]), bf(rw_w2[j]), rw_w0[j].astype(F32), "decay")
            a = _lora(xa, bf(rw_a1[j]), bf(rw_a2[j]), rw_a0[j].astype(F32), "lr")
            g = _lora(xg, bf(rw_g1[j]), bf(rw_g2[j]), jnp.zeros((D,), F32), "gate")
            o = _rwkv_core(r, k, v, lw, a, g, rw_k_k[j], rw_k_a[j], rw_r_k[j].reshape(D),
                           rw_lnx_w[j], rw_lnx_b[j], B=B)
            h = _matmul_wres(o, rw_w_o[j], F32)
        xf, xb = _deepnorm(xf, h, ln_g[i, 0], ln_b[i, 0], alpha)

        kv = _matmul(memb, bf(ca_w_kv[i]), BF16).reshape(B, mem.shape[1], 2 * CA_DIM)
        xf, xb = _cross_attention(xb, xf, kv, bf(ca_w_q[i]), bf(ca_w_o[i]),
                                  ln_g[i, 1], ln_b[i, 1], alpha, B=B)

        act = _ffn_up(xb, ffn_w_up[i].astype(F32), ffn_conv_w[i].astype(F32), ffn_conv_b[i].astype(F32), S=S)
        h = _matmul(act, bf(ffn_w_down[i]), F32, tn=512, tk=tk_down)
        xf, xb = _deepnorm(xf, h, ln_g[i, 2], ln_b[i, 2], alpha)
    return xf.reshape(B, S, D)
```

```python
import functools
import math

import jax
import jax.numpy as jnp
from jax import lax
from jax.experimental import pallas as pl
from jax.experimental.pallas import tpu as pltpu

F32 = jnp.float32
BF16 = jnp.bfloat16

_VMEM_LIMIT_BYTES = 56 * 1024 * 1024
_LANES = 128

DA_HEAD_DIM = 128
N_BUCKETS = 32
MAX_DIST = 128
RW_HEAD = 64
GN_EPS = 64e-5
CA_HEADS = 4
CA_HEAD_DIM = 128
CA_DIM = CA_HEADS * CA_HEAD_DIM
CONV_W = 3
LN_EPS = 1e-5
RW_CHUNK = 64

_NEG = float(jnp.finfo(jnp.float32).min)
_LOG2E = math.log2(math.e)


def _params(sem):
    return pltpu.CompilerParams(dimension_semantics=sem, vmem_limit_bytes=_VMEM_LIMIT_BYTES)


def _pick(n, prefs):
    for p in prefs:
        if n % p == 0:
            return p
    return n


def _mm_kernel(x_ref, w_ref, o_ref, *scratch, nk, scaled_tiles, col_scale):
    part = jnp.dot(x_ref[...], w_ref[...], preferred_element_type=F32)

    def finish(acc):
        if scaled_tiles:
            acc = acc * jnp.where(pl.program_id(1) < scaled_tiles, col_scale, 1.0)
        o_ref[...] = acc.astype(o_ref.dtype)

    if nk == 1:
        finish(part)
        return
    (acc_ref,) = scratch
    k = pl.program_id(2)

    @pl.when(k == 0)
    def _():
        acc_ref[...] = part

    @pl.when(jnp.logical_and(k > 0, k < nk - 1))
    def _():
        acc_ref[...] += part

    @pl.when(k == nk - 1)
    def _():
        finish(acc_ref[...] + part)


def _matmul(x, w, out_dtype, *, tm=1024, tn=1024, tk=None, scaled_cols=0, col_scale=1.0):
    M, K = x.shape
    N = w.shape[1]
    tm = _pick(M, (tm, 512, 256, 128))
    tn = _pick(N, (tn, 512, 256, 128))
    tk = K if tk is None else tk
    nk = K // tk
    assert K % tk == 0 and scaled_cols % tn == 0
    scratch = [pltpu.VMEM((tm, tn), F32)] if nk > 1 else []
    return pl.pallas_call(
        functools.partial(_mm_kernel, nk=nk, scaled_tiles=scaled_cols // tn, col_scale=col_scale),
        out_shape=jax.ShapeDtypeStruct((M, N), out_dtype),
        grid=(M // tm, N // tn, nk),
        in_specs=[pl.BlockSpec((tm, tk), lambda i, j, k: (i, k)),
                  pl.BlockSpec((tk, tn), lambda i, j, k: (k, j))],
        out_specs=pl.BlockSpec((tm, tn), lambda i, j, k: (i, j)),
        scratch_shapes=scratch,
        compiler_params=_params(("parallel", "parallel", "arbitrary")),
        name="matmul",
    )(x, w)


def _layer_norm(z, g, b):
    mu = jnp.mean(z, axis=-1, keepdims=True)
    zc = z - mu
    var = jnp.mean(zc * zc, axis=-1, keepdims=True)
    return zc * lax.rsqrt(var + LN_EPS) * g + b


def _ln_kernel(x_ref, h_ref, g_ref, b_ref, o_ref, ob_ref, *, alpha):
    y = _layer_norm(alpha * x_ref[...] + h_ref[...], g_ref[...], b_ref[...])
    o_ref[...] = y
    ob_ref[...] = y.astype(BF16)


def _deepnorm(x, h, g, b, alpha):
    M, D = x.shape
    tm = _pick(M, (256, 128))
    row = pl.BlockSpec((tm, D), lambda i: (i, 0))
    vec = pl.BlockSpec((1, D), lambda i: (0, 0))
    return pl.pallas_call(
        functools.partial(_ln_kernel, alpha=alpha),
        out_shape=(jax.ShapeDtypeStruct((M, D), F32), jax.ShapeDtypeStruct((M, D), BF16)),
        grid=(M // tm,),
        in_specs=[row, row, vec, vec],
        out_specs=(row, row),
        compiler_params=_params(("parallel",)),
        name="deepnorm",
    )(x, h, g.reshape(1, D), b.reshape(1, D))


def _t5_buckets(n_dist):
    max_exact = N_BUCKETS // 2
    out = []
    for n in range(n_dist):
        if n < max_exact:
            out.append(n)
            continue
        val = math.log(n / max_exact) / math.log(MAX_DIST / max_exact) * (N_BUCKETS - max_exact)
        assert n == max_exact or n >= MAX_DIST or abs(val - round(val)) > 1e-3
        out.append(min(max_exact + int(val), N_BUCKETS - 1))
    return out


def _attn_kernel(qmin_ref, qmax_ref, kmin_ref, kmax_ref, last_ref, nfar_ref,
                 q_ref, k_ref, v_ref, qp_ref, kp_ref, tbl_ref, lam_ref, sub_ref,
                 o_ref, m_ref, l_ref, acc_ref, *, tk, tks, far_dist, lam_init):
    b = pl.program_id(0)
    i = pl.program_id(2)
    d = DA_HEAD_DIM
    tq = q_ref.shape[0]
    qmn = qmin_ref[b, i]
    qmx = qmax_ref[b, i]
    far_bias = tbl_ref[:, _LANES - 1:_LANES]

    m_ref[...] = jnp.full(m_ref.shape, _NEG, F32)
    l_ref[...] = jnp.zeros(l_ref.shape, F32)
    acc_ref[...] = jnp.zeros(acc_ref.shape, F32)

    def scores(r0, width):
        kblk = k_ref[pl.ds(r0, width), :]
        return [lax.dot_general(q_ref[:, mi * d:(mi + 1) * d], kblk[:, mi * d:(mi + 1) * d],
                                (((1,), (1,)), ((), ())), preferred_element_type=F32)
                for mi in range(2)]

    def update(r0, width, s, near):
        nrep = width // _LANES
        vblk = v_ref[pl.ds(r0, width), :]
        if near:
            n = qp_ref[...] - kp_ref[:, pl.ds(r0, width)]
            idx = jnp.clip(n, 0, _LANES - 1)
            table = jnp.broadcast_to(tbl_ref[...], (tq, _LANES))
            bias = jnp.concatenate(
                [jnp.take_along_axis(table, idx[:, c * _LANES:(c + 1) * _LANES], axis=1)
                 for c in range(nrep)], axis=1)
            keep = n >= 0
            s = [jnp.where(keep, sm + bias, _NEG) for sm in s]
            shift_bias = 0.0
        else:
            shift_bias = far_bias
        maps = range(2)
        m_prev = [m_ref[mi] for mi in maps]
        m_new = [jnp.maximum(m_prev[mi], jnp.max(s[mi], axis=-1, keepdims=True) + shift_bias)
                 for mi in maps]
        alpha = [jnp.exp2(m_prev[mi] - m_new[mi]) for mi in maps]
        p = [jnp.exp2(s[mi] - jnp.tile(m_new[mi] - shift_bias, (1, nrep))) for mi in maps]
        for mi in maps:
            psum = p[mi][:, :_LANES]
            for c in range(1, nrep):
                psum = psum + p[mi][:, c * _LANES:(c + 1) * _LANES]
            l_ref[mi] = alpha[mi] * l_ref[mi] + psum
            m_ref[mi] = m_new[mi]
        pv = [jnp.dot(p[mi].astype(BF16), vblk, preferred_element_type=F32) for mi in maps]
        for mi in maps:
            acc_ref[mi] = jnp.tile(alpha[mi], (1, 2 * d // _LANES)) * acc_ref[mi] + pv[mi]

    def far_group(j0, n):
        start = lambda u: pl.multiple_of((j0 + u) * tk, tk)
        s = scores(start(0), tk)
        for u in range(n):
            s_next = scores(start(u + 1), tk) if u + 1 < n else None
            update(start(u), tk, s, False)
            s = s_next

    def quad_body(t, carry):
        far_group(4 * t, 4)
        return carry

    nfar = nfar_ref[b, i]
    nquad = nfar // 4
    lax.fori_loop(0, nquad, quad_body, 0)
    rem_pair = (nfar - 4 * nquad) // 2

    @pl.when(rem_pair == 1)
    def _():
        far_group(4 * nquad, 2)

    ngrouped = 4 * nquad + 2 * rem_pair

    def single_body(j, carry):
        r0 = pl.multiple_of(j * tks, tks)
        active = kmin_ref[b, j] <= qmx
        far = (qmn - kmax_ref[b, j]) >= far_dist

        @pl.when(jnp.logical_and(active, far))
        def _():
            update(r0, tks, scores(r0, tks), False)

        @pl.when(jnp.logical_and(active, jnp.logical_not(far)))
        def _():
            update(r0, tks, scores(r0, tks), True)

        return carry

    lax.fori_loop(ngrouped * (tk // tks), last_ref[b, i] + 1, single_body, 0)

    lv = lam_ref[...]
    lam = (jnp.exp(jnp.sum(lv[0:1] * lv[1:2], axis=-1, keepdims=True))
           - jnp.exp(jnp.sum(lv[2:3] * lv[3:4], axis=-1, keepdims=True)) + lam_init)
    l1 = jnp.sum(l_ref[0], axis=-1, keepdims=True)
    l2 = jnp.sum(l_ref[1], axis=-1, keepdims=True)
    o = acc_ref[0] / l1 - lam * (acc_ref[1] / l2)
    o = o * lax.rsqrt(jnp.mean(o * o, axis=-1, keepdims=True) + LN_EPS)
    o_ref[...] = (o * sub_ref[...] * (1.0 - lam_init)).astype(o_ref.dtype)


def _diff_attention(qkv, positions, rel_bias, lam_vecs, subln_w, lam_init, *, tq=512, tk=512, tks=512):
    B, S, D3 = qkv.shape
    D = D3 // 3
    H = D // (2 * DA_HEAD_DIM)
    w = 2 * DA_HEAD_DIM
    tq = _pick(S, (tq, 256, 128))
    tk = _pick(S, (tk, 256, 128))
    tks = min(tks, tk)
    nq, nk, nks = S // tq, S // tk, S // tks
    buckets = _t5_buckets(_LANES)
    assert buckets[-1] == N_BUCKETS - 1
    far_dist = buckets.index(N_BUCKETS - 1)
    table = rel_bias.astype(F32)[jnp.array(buckets, jnp.int32), :].T * _LOG2E
    table = table.reshape(H, 1, _LANES)

    pq = positions.reshape(B, nq, tq)
    pk = positions.reshape(B, nks, tks)
    qmin, qmax = pq.min(-1), pq.max(-1)
    kmin, kmax = pk.min(-1), pk.max(-1)
    needed = kmin[:, None, :] <= qmax[:, :, None]
    last = jnp.max(jnp.where(needed, jnp.arange(nks, dtype=jnp.int32), 0), axis=-1).astype(jnp.int32)
    kmax_wide = kmax.reshape(B, nk, tk // tks).max(-1)
    far = (qmin[:, :, None] - kmax_wide[:, None, :]) >= far_dist
    nfar = jnp.sum(jnp.cumprod(far.astype(jnp.int32), axis=-1), axis=-1).astype(jnp.int32)
    nfar = jnp.minimum(nfar, (last + 1) // (tk // tks))

    grid_spec = pltpu.PrefetchScalarGridSpec(
        num_scalar_prefetch=6,
        grid=(B, H, nq),
        in_specs=[
            pl.BlockSpec((None, tq, w), lambda b, h, i, *_: (b, i, h)),
            pl.BlockSpec((None, S, w), lambda b, h, i, *_: (b, 0, H + h)),
            pl.BlockSpec((None, S, w), lambda b, h, i, *_: (b, 0, 2 * H + h)),
            pl.BlockSpec((None, tq, 1), lambda b, h, i, *_: (b, i, 0)),
            pl.BlockSpec((None, 1, S), lambda b, h, i, *_: (b, 0, 0)),
            pl.BlockSpec((None, 1, _LANES), lambda b, h, i, *_: (h, 0, 0)),
            pl.BlockSpec((4, DA_HEAD_DIM), lambda b, h, i, *_: (0, 0)),
            pl.BlockSpec((1, w), lambda b, h, i, *_: (0, 0)),
        ],
        out_specs=pl.BlockSpec((None, tq, w), lambda b, h, i, *_: (b, i, h)),
        scratch_shapes=[pltpu.VMEM((2, tq, _LANES), F32), pltpu.VMEM((2, tq, _LANES), F32),
                        pltpu.VMEM((2, tq, w), F32)],
    )
    return pl.pallas_call(
        functools.partial(_attn_kernel, tk=tk, tks=tks, far_dist=far_dist, lam_init=lam_init),
        out_shape=jax.ShapeDtypeStruct((B, S, D), BF16),
        grid_spec=grid_spec,
        compiler_params=_params(("parallel", "parallel", "arbitrary")),
        name="diff_attention",
    )(qmin, qmax, kmin, kmax, last, nfar,
      qkv, qkv, qkv, positions.reshape(B, S, 1), positions.reshape(B, 1, S),
      table, lam_vecs.astype(F32), subln_w.astype(F32).reshape(1, w))


def _xattn_kernel(xb_ref, xf_ref, wq_ref, kv_ref, wo_ref, g_ref, b_ref, o_ref, ob_ref, *, alpha):
    dh = CA_HEAD_DIM
    q = jnp.dot(xb_ref[...], wq_ref[...], preferred_element_type=F32) * (dh ** -0.5)
    q = q.astype(BF16)
    heads = []
    for hh in range(CA_HEADS):
        kh = kv_ref[:, hh * dh:(hh + 1) * dh]
        vh = kv_ref[:, CA_DIM + hh * dh:CA_DIM + (hh + 1) * dh]
        s = lax.dot_general(q[:, hh * dh:(hh + 1) * dh], kh, (((1,), (1,)), ((), ())),
                            preferred_element_type=F32)
        p = jnp.exp(s - jnp.max(s, axis=-1, keepdims=True))
        p = p / jnp.sum(p, axis=-1, keepdims=True)
        heads.append(jnp.dot(p.astype(BF16), vh, preferred_element_type=F32))
    o = jnp.concatenate(heads, axis=-1).astype(BF16)
    hcat = jnp.dot(o, wo_ref[...], preferred_element_type=F32)
    y = _layer_norm(alpha * xf_ref[...] + hcat, g_ref[...], b_ref[...])
    o_ref[...] = y
    ob_ref[...] = y.astype(BF16)


def _cross_attention(xb, xf, kv, w_q, w_o, g, b, alpha, *, B):
    M, D = xf.shape
    S = M // B
    NM = kv.shape[1]
    tm = _pick(S, (256, 128))
    nt = S // tm
    row = pl.BlockSpec((tm, D), lambda i: (i, 0))
    vec = pl.BlockSpec((1, D), lambda i: (0, 0))
    return pl.pallas_call(
        functools.partial(_xattn_kernel, alpha=alpha),
        out_shape=(jax.ShapeDtypeStruct((M, D), F32), jax.ShapeDtypeStruct((M, D), BF16)),
        grid=(M // tm,),
        in_specs=[row, row,
                  pl.BlockSpec((D, CA_DIM), lambda i: (0, 0)),
                  pl.BlockSpec((None, NM, 2 * CA_DIM), lambda i: (i // nt, 0, 0)),
                  pl.BlockSpec((CA_DIM, D), lambda i: (0, 0)),
                  vec, vec],
        out_specs=(row, row),
        compiler_params=_params(("parallel",)),
        name="cross_attention",
    )(xb, xf, w_q, kv, w_o, g.reshape(1, D), b.reshape(1, D))


def _ffn_up_kernel(x_ref, wg_ref, wv_ref, cwg_ref, cwv_ref, cbg_ref, cbv_ref, o_ref,
                   carry_ref, *, ts, tiles_per_seq):
    i = pl.program_id(1)
    nsub = x_ref.shape[0] // ts
    row = lax.broadcasted_iota(jnp.int32, (ts, 1), 0)
    wg = wg_ref[...]
    wv = wv_ref[...]

    def up(u):
        x = x_ref[u * ts:(u + 1) * ts, :]
        return (jnp.dot(x, wg, preferred_element_type=F32), jnp.dot(x, wv, preferred_element_type=F32))

    def conv(hcur, prev, cw_ref, cb_ref):
        p1 = prev[7:8, :]
        p2 = prev[6:7, :]
        h1 = jnp.where(row == 0, p1, pltpu.roll(hcur, 1, axis=0))
        h2 = jnp.where(row == 0, p2, jnp.where(row == 1, p1, pltpu.roll(hcur, 2, axis=0)))
        return cw_ref[0:1, :] * h2 + cw_ref[1:2, :] * h1 + cw_ref[2:3, :] * hcur + cb_ref[...]

    prev = (carry_ref[0], carry_ref[1])
    raw = up(0)
    for u in range(nsub):
        raw_next = up(u + 1) if u + 1 < nsub else None
        seq_start = ((i * nsub + u) % tiles_per_seq) == 0
        gate = conv(raw[0], jnp.where(seq_start, 0.0, prev[0]), cwg_ref, cbg_ref)
        val = conv(raw[1], jnp.where(seq_start, 0.0, prev[1]), cwv_ref, cbv_ref)
        o_ref[u * ts:(u + 1) * ts, :] = (gate * jax.nn.sigmoid(gate) * val).astype(o_ref.dtype)
        prev = (raw[0][ts - 8:, :], raw[1][ts - 8:, :])
        raw = raw_next
    carry_ref[0] = prev[0]
    carry_ref[1] = prev[1]


def _ffn_up(xb, w_up, conv_w, conv_b, *, S):
    M, D = xb.shape
    F2 = w_up.shape[1]
    F = F2 // 2
    tn = 256
    assert F % tn == 0
    nf = F // tn
    ts = _pick(S, (1024, 512, 256, 128))
    tm = 2 * ts if M % (2 * ts) == 0 else ts
    conv_b = conv_b.reshape(1, F2)
    return pl.pallas_call(
        functools.partial(_ffn_up_kernel, ts=ts, tiles_per_seq=S // ts),
        out_shape=jax.ShapeDtypeStruct((M, F), BF16),
        grid=(nf, M // tm),
        in_specs=[pl.BlockSpec((tm, D), lambda j, i: (i, 0)),
                  pl.BlockSpec((D, tn), lambda j, i: (0, j)),
                  pl.BlockSpec((D, tn), lambda j, i: (0, nf + j)),
                  pl.BlockSpec((CONV_W, tn), lambda j, i: (0, j)),
                  pl.BlockSpec((CONV_W, tn), lambda j, i: (0, nf + j)),
                  pl.BlockSpec((1, tn), lambda j, i: (0, j)),
                  pl.BlockSpec((1, tn), lambda j, i: (0, nf + j))],
        out_specs=pl.BlockSpec((tm, tn), lambda j, i: (i, j)),
        scratch_shapes=[pltpu.VMEM((2, 8, tn), F32)],
        compiler_params=_params(("parallel", "arbitrary")),
        name="ffn_up_conv_glu",
    )(xb, w_up, w_up, conv_w, conv_w, conv_b, conv_b)


def _mix_kernel(x_ref, mix_ref, *rest, tiles_per_seq):
    outs, carry_ref = rest[:-1], rest[-1]
    i = pl.program_id(0)
    tm = x_ref.shape[0]
    x = x_ref[...]
    row = lax.broadcasted_iota(jnp.int32, (tm, 1), 0)
    prev_last = jnp.where((i % tiles_per_seq) == 0, 0.0, carry_ref[7:8, :])
    xx = jnp.where(row == 0, prev_last, pltpu.roll(x, 1, axis=0)) - x
    carry_ref[...] = x[tm - 8:, :]
    for m, o_ref in enumerate(outs):
        o_ref[...] = (x + xx * mix_ref[m:m + 1, :]).astype(BF16)


def _token_shift_mix(x, mix, *, S):
    M, D = x.shape
    tm = _pick(S, (256, 128))
    n = mix.shape[0]
    row = pl.BlockSpec((tm, D), lambda i: (i, 0))
    return pl.pallas_call(
        functools.partial(_mix_kernel, tiles_per_seq=S // tm),
        out_shape=tuple(jax.ShapeDtypeStruct((M, D), BF16) for _ in range(n)),
        grid=(M // tm,),
        in_specs=[row, pl.BlockSpec((n, D), lambda i: (0, 0))],
        out_specs=tuple(row for _ in range(n)),
        scratch_shapes=[pltpu.VMEM((8, D), F32)],
        compiler_params=_params(("arbitrary",)),
        name="token_shift_mix",
    )(x, mix)


def _lora_kernel(x_ref, w1_ref, w2_ref, b_ref, o_ref, *, mode):
    t = jnp.dot(x_ref[...], w1_ref[...], preferred_element_type=F32)
    if mode == "decay":
        t = jnp.tanh(t)
    elif mode == "gate":
        t = jax.nn.sigmoid(t)
    z = jnp.dot(t.astype(BF16), w2_ref[...], preferred_element_type=F32) + b_ref[...]
    if mode == "decay":
        u = -z
        sp = jnp.maximum(u, 0.0) + jnp.log(1.0 + jnp.exp(-jnp.abs(u)))
        z = -jnp.exp(-sp - 0.5)
    elif mode == "lr":
        z = jax.nn.sigmoid(z)
    o_ref[...] = z


def _lora(xb, w1, w2, bias, mode):
    M, D = xb.shape
    R = w1.shape[1]
    tm = _pick(M, (512, 256, 128))
    row = pl.BlockSpec((tm, D), lambda i: (i, 0))
    return pl.pallas_call(
        functools.partial(_lora_kernel, mode=mode),
        out_shape=jax.ShapeDtypeStruct((M, D), F32),
        grid=(M // tm,),
        in_specs=[row, pl.BlockSpec((D, R), lambda i: (0, 0)),
                  pl.BlockSpec((R, D), lambda i: (0, 0)),
                  pl.BlockSpec((1, D), lambda i: (0, 0))],
        out_specs=row,
        compiler_params=_params(("parallel",)),
        name="lora_" + mode,
    )(xb, w1, w2, bias.reshape(1, D))


def _dot_bf(a, b):
    return jnp.dot(a.astype(BF16), b.astype(BF16), preferred_element_type=F32)


def _split3(x):
    hi = x.astype(BF16)
    r1 = x - hi.astype(F32)
    mid = r1.astype(BF16)
    lo = (r1 - mid.astype(F32)).astype(BF16)
    return hi, mid, lo


def _dot_exact_rhs(x, w_bf):
    n = x.shape[0]
    r = jnp.dot(jnp.concatenate(_split3(x), axis=0), w_bf, preferred_element_type=F32)
    return r[:n] + r[n:2 * n] + r[2 * n:]


def _dot_exact_lhs(w_bf, x):
    n = x.shape[1]
    r = jnp.dot(w_bf, jnp.concatenate(_split3(x), axis=1), preferred_element_type=F32)
    return r[:, :n] + r[:, n:2 * n] + r[:, 2 * n:]


def _dot_x3(a, b):
    n = a.shape[0]
    a_hi = a.astype(BF16)
    a_lo = (a - a_hi.astype(F32)).astype(BF16)
    b_hi = b.astype(BF16)
    b_lo = (b - b_hi.astype(F32)).astype(BF16)
    r = jnp.dot(jnp.concatenate([a_hi, a_lo], axis=0), b_hi, preferred_element_type=F32)
    return r[:n] + r[n:] + jnp.dot(a_hi, b_lo, preferred_element_type=F32)


def _head_ones():
    r = lax.broadcasted_iota(jnp.int32, (_LANES, _LANES), 0) // RW_HEAD
    c = lax.broadcasted_iota(jnp.int32, (_LANES, _LANES), 1) // RW_HEAD
    return jnp.where(r == c, 1.0, 0.0).astype(BF16)


def _rwkv_prep_kernel(r_ref, k_ref, v_ref, lw_ref, a_ref, kk_ref, ka_ref, rk_ref, lb_ref,
                      rp_ref, y0_ref, e_ref, g_ref, h_ref, *, npairs):
    L = RW_CHUNK
    n2 = 2 * L
    pairs = range(npairs)
    sls = [slice(p * _LANES, (p + 1) * _LANES) for p in pairs]
    ones_bd = _head_ones()
    ri = lax.broadcasted_iota(jnp.int32, (n2, n2), 0)
    ci = lax.broadcasted_iota(jnp.int32, (n2, n2), 1)
    strict = ci < ri
    incl = ci <= ri
    diag = ri == ci
    tri = jnp.where(lax.broadcasted_iota(jnp.int32, (L, L), 1)
                    <= lax.broadcasted_iota(jnp.int32, (L, L), 0), 1.0, 0.0).astype(BF16)
    head0 = lax.broadcasted_iota(jnp.int32, (L, _LANES), 1) < RW_HEAD

    def stack(x):
        return jnp.concatenate([jnp.where(head0, x, 0.0), jnp.where(head0, 0.0, x)], axis=0)

    def unstack(x):
        return x[:L] + x[L:]

    r = [r_ref[:, sl] for sl in sls]
    k = [k_ref[:, sl] for sl in sls]
    v = [v_ref[:, sl] for sl in sls]
    lw = [lw_ref[:, sl] for sl in sls]
    a = [a_ref[:, sl] for sl in sls]
    kk = [k[p] * kk_ref[:, sls[p]] for p in pairs]
    kmod = [k[p] * (1.0 + (a[p] - 1.0) * ka_ref[:, sls[p]]) for p in pairs]
    cl = [_dot_exact_lhs(tri, lw[p]) for p in pairs]
    hs = [_dot_exact_rhs(jnp.concatenate([kk[p] * kk[p], r[p] * kmod[p] * rk_ref[:, sls[p]]], axis=0),
                         ones_bd) for p in pairs]
    for p in pairs:
        e_ref[:, sls[p]] = lb_ref[:, sls[p]] + hs[p][L:] * v[p]
    kk = [kk[p] / jnp.maximum(jnp.sqrt(hs[p][:L]), 1e-12) for p in pairs]
    bv = [kk[p] * a[p] for p in pairs]
    cl_end = [c[L - 1:L, :] for c in cl]
    c_inv = [jnp.exp(-c) for c in cl]
    c_rem = [jnp.exp(cl_end[p] - cl[p]) for p in pairs]
    a_s = [stack(-kk[p] * jnp.exp(cl[p] - lw[p])) for p in pairs]
    r_s = [stack(r[p] * jnp.exp(cl[p])) for p in pairs]
    v_s = [stack(v[p]).astype(BF16) for p in pairs]
    bk_t = [jnp.concatenate([stack(bv[p] * c_inv[p]), stack(kmod[p] * c_inv[p])], axis=0)
            .astype(BF16).T for p in pairs]
    ar_bk = [_dot_bf(jnp.concatenate([a_s[p], r_s[p]], axis=0), bk_t[p]) for p in pairs]
    nab = [jnp.where(strict, m[:n2, :n2], 0.0) for m in ar_bk]
    nak = [jnp.where(strict, m[:n2, n2:], 0.0) for m in ar_bk]
    mrb = [jnp.where(incl, m[n2:, :n2], 0.0) for m in ar_bk]
    mrk = [jnp.where(incl, m[n2:, n2:], 0.0) for m in ar_bk]
    xv = [_dot_bf(jnp.concatenate([nak[p], mrk[p], stack(kmod[p] * c_rem[p]).T], axis=0), v_s[p])
          for p in pairs]
    levels = int(math.log2(L)) - 1
    t_inv = [jnp.where(diag, 1.0, m) for m in nab]
    x = [_dot_bf(m, m) for m in nab]
    for lvl in range(levels):
        if lvl < levels - 1:
            tx = [_dot_bf(jnp.concatenate([t_inv[p], x[p]], axis=0), x[p]) for p in pairs]
            t_inv = [t_inv[p] + tx[p][:n2] for p in pairs]
            x = [tx[p][n2:] for p in pairs]
        else:
            t_inv = [t_inv[p] + _dot_bf(t_inv[p], x[p]) for p in pairs]
    pq = [_dot_bf(t_inv[p], jnp.concatenate([a_s[p], xv[p][:n2]], axis=1)) for p in pairs]
    rg = [_dot_bf(jnp.concatenate([mrb[p], stack(bv[p] * c_rem[p]).T], axis=0), pq[p]) for p in pairs]
    for p in pairs:
        rp_ref[:, sls[p]] = unstack(r_s[p] + rg[p][:n2, :_LANES])
        y0_ref[:, sls[p]] = unstack(rg[p][:n2, _LANES:] + xv[p][n2:2 * n2])
        g_ref[p] = jnp.where(diag, jnp.exp(cl_end[p]), 0.0) + rg[p][n2:, :_LANES]
        h_ref[p] = rg[p][n2:, _LANES:] + xv[p][2 * n2:]


def _rwkv_scan_kernel(rp_ref, y0_ref, e_ref, gate_ref, g_ref, h_ref, lnw_ref, o_ref, st_ref, *, npairs):
    c = pl.program_id(2)

    @pl.when(c == 0)
    def _():
        st_ref[...] = jnp.zeros(st_ref.shape, F32)

    ones_bd = _head_ones()
    inv_n = 1.0 / RW_HEAD
    pairs = range(npairs)
    sls = [slice(p * _LANES, (p + 1) * _LANES) for p in pairs]
    prod = [_dot_x3(jnp.concatenate([g_ref[p], rp_ref[:, sls[p]]], axis=0), st_ref[p]) for p in pairs]
    for p in pairs:
        st_ref[p] = prod[p][:_LANES] + h_ref[p]
    y = [prod[p][_LANES:] + y0_ref[:, sls[p]] for p in pairs]
    mu = [_dot_exact_rhs(y[p], ones_bd) * inv_n for p in pairs]
    yc = [y[p] - mu[p] for p in pairs]
    var = [_dot_exact_rhs(yc[p] * yc[p], ones_bd) * inv_n for p in pairs]
    for p in pairs:
        yn = yc[p] * lax.rsqrt(var[p] + GN_EPS)
        o_ref[:, sls[p]] = ((yn * lnw_ref[:, sls[p]] + e_ref[:, sls[p]])
                            * gate_ref[:, sls[p]]).astype(o_ref.dtype)


def _rwkv_core(r, k, v, lw, a, gate, k_k, k_a, r_k, lnx_w, lnx_b, *, B):
    M, D = r.shape
    S = M // B
    L = RW_CHUNK
    nc = S // L
    ngroups = D // _LANES
    np1 = _pick(ngroups, (8, 4, 2, 1))
    np2 = _pick(ngroups, (16, 8, 4, 2, 1))
    vec = lambda t: t.astype(F32).reshape(1, D)

    w1 = np1 * _LANES
    tile1 = pl.BlockSpec((L, w1), lambda b, c, g: (b * nc + c, g))
    par1 = pl.BlockSpec((1, w1), lambda b, c, g: (0, g))
    mat1 = pl.BlockSpec((None, None, np1, _LANES, _LANES), lambda b, c, g: (b, c, g, 0, 0))
    md = jax.ShapeDtypeStruct((M, D), F32)
    gh = jax.ShapeDtypeStruct((B, nc, ngroups, _LANES, _LANES), F32)
    rp, y0, e, gm, hm = pl.pallas_call(
        functools.partial(_rwkv_prep_kernel, npairs=np1),
        out_shape=(md, md, md, gh, gh),
        grid=(B, nc, ngroups // np1),
        in_specs=[tile1] * 5 + [par1] * 4,
        out_specs=(tile1, tile1, tile1, mat1, mat1),
        compiler_params=_params(("parallel", "parallel", "parallel")),
        name="rwkv_chunk_prep",
    )(r, k, v, lw, a, vec(k_k), vec(k_a), vec(r_k), vec(lnx_b))

    w2 = np2 * _LANES
    tile2 = pl.BlockSpec((L, w2), lambda b, g, c: (b * nc + c, g))
    par2 = pl.BlockSpec((1, w2), lambda b, g, c: (0, g))
    mat2 = pl.BlockSpec((None, None, np2, _LANES, _LANES), lambda b, g, c: (b, c, g, 0, 0))
    return pl.pallas_call(
        functools.partial(_rwkv_scan_kernel, npairs=np2),
        out_shape=jax.ShapeDtypeStruct((M, D), BF16),
        grid=(B, ngroups // np2, nc),
        in_specs=[tile2] * 4 + [mat2, mat2, par2],
        out_specs=tile2,
        scratch_shapes=[pltpu.VMEM((np2, _LANES, _LANES), F32)],
        compiler_params=_params(("parallel", "parallel", "arbitrary")),
        name="rwkv_state_scan",
    )(rp, y0, e, gate, gm, hm, vec(lnx_w))


def kernel(x, mem, positions, rel_bias, da_w_qkv, da_lam, da_subln, da_w_o, rw_mix, rw_w_rkv, rw_w0, rw_w1, rw_w2, rw_a0, rw_a1, rw_a2, rw_g1, rw_g2, rw_k_k, rw_k_a, rw_r_k, rw_lnx_w, rw_lnx_b, rw_w_o, ca_w_q, ca_w_kv, ca_w_o, ffn_w_up, ffn_conv_w, ffn_conv_b, ffn_w_down, ln_g, ln_b):
    B, S, D = x.shape
    M = B * S
    depth = ln_g.shape[0]
    alpha = (2 * depth) ** 0.25
    bf = lambda t: t.astype(BF16)
    memb = bf(mem).reshape(B * mem.shape[1], D)
    F = ffn_w_down.shape[1]
    tk_down = F // 2 if (F // 2) % _LANES == 0 else F

    xf = x.reshape(M, D)
    xb = bf(xf)
    for i in range(depth):
        j = i // 2
        if i % 2 == 0:
            lam_init = 0.8 - 0.6 * math.exp(-0.3 * i)
            qkv = _matmul(xb, bf(da_w_qkv[j]), BF16, scaled_cols=D,
                          col_scale=DA_HEAD_DIM ** -0.5 * _LOG2E)
            att = _diff_attention(qkv.reshape(B, S, 3 * D), positions, rel_bias, da_lam[j],
                                  da_subln[j], lam_init)
            h = _matmul(att.reshape(M, D), bf(da_w_o[j]), F32)
        else:
            xr, xw, xk, xv, xa, xg = _token_shift_mix(xf, rw_mix[j].astype(F32), S=S)
            r = _matmul(xr, bf(rw_w_rkv[j, 0]), F32)
            k = _matmul(xk, bf(rw_w_rkv[j, 1]), F32)
            v = _matmul(xv, bf(rw_w_rkv[j, 2]), F32)
            lw = _lora(xw, bf(rw_w1[j]), bf(rw_w2[j]), rw_w0[j].astype(F32), "decay")
            a = _lora(xa, bf(rw_a1[j]), bf(rw_a2[j]), rw_a0[j].astype(F32), "lr")
            g = _lora(xg, bf(rw_g1[j]), bf(rw_g2[j]), jnp.zeros((D,), F32), "gate")
            o = _rwkv_core(r, k, v, lw, a, g, rw_k_k[j], rw_k_a[j], rw_r_k[j].reshape(D),
                           rw_lnx_w[j], rw_lnx_b[j], B=B)
            h = _matmul(o, bf(rw_w_o[j]), F32)
        xf, xb = _deepnorm(xf, h, ln_g[i, 0], ln_b[i, 0], alpha)

        kv = _matmul(memb, bf(ca_w_kv[i]), BF16).reshape(B, mem.shape[1], 2 * CA_DIM)
        xf, xb = _cross_attention(xb, xf, kv, bf(ca_w_q[i]), bf(ca_w_o[i]),
                                  ln_g[i, 1], ln_b[i, 1], alpha, B=B)

        act = _ffn_up(xb, bf(ffn_w_up[i]), ffn_conv_w[i].astype(F32), ffn_conv_b[i].astype(F32), S=S)
        h = _matmul(act, bf(ffn_w_down[i]), F32, tn=512, tk=tk_down)
        xf, xb = _deepnorm(xf, h, ln_g[i, 2], ln_b[i, 2], alpha)
    return xf.reshape(B, S, D)
```

```python
import functools
import math

import jax
import jax.numpy as jnp
from jax import lax
from jax.experimental import pallas as pl
from jax.experimental.pallas import tpu as pltpu

F32 = jnp.float32
BF16 = jnp.bfloat16

_VMEM_LIMIT_BYTES = 56 * 1024 * 1024
_LANES = 128

DA_HEAD_DIM = 128
N_BUCKETS = 32
MAX_DIST = 128
RW_HEAD = 64
GN_EPS = 64e-5
CA_HEADS = 4
CA_HEAD_DIM = 128
CA_DIM = CA_HEADS * CA_HEAD_DIM
CONV_W = 3
LN_EPS = 1e-5
RW_CHUNK = 64

_ATTN_TRIP_KEYS = 2048
_NEG = float(jnp.finfo(jnp.float32).min)
_LOG2E = math.log2(math.e)


def _params(sem):
    return pltpu.CompilerParams(dimension_semantics=sem, vmem_limit_bytes=_VMEM_LIMIT_BYTES)


def _pick(n, prefs):
    for p in prefs:
        if n % p == 0:
            return p
    return n


def _mm_kernel(x_ref, w_ref, o_ref, *scratch, nk, scaled_tiles, col_scale):
    part = jnp.dot(x_ref[...], w_ref[...], preferred_element_type=F32)

    def finish(acc):
        if scaled_tiles:
            acc = acc * jnp.where(pl.program_id(1) < scaled_tiles, col_scale, 1.0)
        o_ref[...] = acc.astype(o_ref.dtype)

    if nk == 1:
        finish(part)
        return
    (acc_ref,) = scratch
    k = pl.program_id(2)

    @pl.when(k == 0)
    def _():
        acc_ref[...] = part

    @pl.when(jnp.logical_and(k > 0, k < nk - 1))
    def _():
        acc_ref[...] += part

    @pl.when(k == nk - 1)
    def _():
        finish(acc_ref[...] + part)


def _matmul(x, w, out_dtype, *, tm=1024, tn=1024, tk=None, scaled_cols=0, col_scale=1.0):
    M, K = x.shape
    N = w.shape[1]
    tm = _pick(M, (tm, 512, 256, 128))
    tn = _pick(N, (tn, 512, 256, 128))
    tk = K if tk is None else tk
    nk = K // tk
    assert K % tk == 0 and scaled_cols % tn == 0
    scratch = [pltpu.VMEM((tm, tn), F32)] if nk > 1 else []
    return pl.pallas_call(
        functools.partial(_mm_kernel, nk=nk, scaled_tiles=scaled_cols // tn, col_scale=col_scale),
        out_shape=jax.ShapeDtypeStruct((M, N), out_dtype),
        grid=(M // tm, N // tn, nk),
        in_specs=[pl.BlockSpec((tm, tk), lambda i, j, k: (i, k)),
                  pl.BlockSpec((tk, tn), lambda i, j, k: (k, j))],
        out_specs=pl.BlockSpec((tm, tn), lambda i, j, k: (i, j)),
        scratch_shapes=scratch,
        compiler_params=_params(("parallel", "parallel", "arbitrary")),
        name="matmul",
    )(x, w)


def _layer_norm(z, g, b):
    mu = jnp.mean(z, axis=-1, keepdims=True)
    zc = z - mu
    var = jnp.mean(zc * zc, axis=-1, keepdims=True)
    return zc * lax.rsqrt(var + LN_EPS) * g + b


def _ln_kernel(x_ref, h_ref, g_ref, b_ref, o_ref, ob_ref, *, alpha):
    y = _layer_norm(alpha * x_ref[...] + h_ref[...], g_ref[...], b_ref[...])
    o_ref[...] = y
    ob_ref[...] = y.astype(BF16)


def _deepnorm(x, h, g, b, alpha):
    M, D = x.shape
    tm = _pick(M, (256, 128))
    row = pl.BlockSpec((tm, D), lambda i: (i, 0))
    vec = pl.BlockSpec((1, D), lambda i: (0, 0))
    return pl.pallas_call(
        functools.partial(_ln_kernel, alpha=alpha),
        out_shape=(jax.ShapeDtypeStruct((M, D), F32), jax.ShapeDtypeStruct((M, D), BF16)),
        grid=(M // tm,),
        in_specs=[row, row, vec, vec],
        out_specs=(row, row),
        compiler_params=_params(("parallel",)),
        name="deepnorm",
    )(x, h, g.reshape(1, D), b.reshape(1, D))


def _t5_buckets(n_dist):
    max_exact = N_BUCKETS // 2
    out = []
    for n in range(n_dist):
        if n < max_exact:
            out.append(n)
            continue
        val = math.log(n / max_exact) / math.log(MAX_DIST / max_exact) * (N_BUCKETS - max_exact)
        assert n == max_exact or n >= MAX_DIST or abs(val - round(val)) > 1e-3
        out.append(min(max_exact + int(val), N_BUCKETS - 1))
    return out


def _attn_kernel(qmin_ref, qmax_ref, kmin_ref, kmax_ref, last_ref, nfar_ref,
                 q_ref, k_ref, v_ref, qp_ref, kp_ref, tbl_ref, lam_ref, sub_ref,
                 o_ref, m_ref, l_ref, acc_ref, *, tk, tks, group, far_dist, lam_init):
    b = pl.program_id(0)
    i = pl.program_id(2)
    d = DA_HEAD_DIM
    tq = q_ref.shape[0]
    qmn = qmin_ref[b, i]
    qmx = qmax_ref[b, i]
    far_bias = tbl_ref[:, _LANES - 1:_LANES]

    m_ref[...] = jnp.full(m_ref.shape, _NEG, F32)
    l_ref[...] = jnp.zeros(l_ref.shape, F32)
    acc_ref[...] = jnp.zeros(acc_ref.shape, F32)

    def scores(r0, width):
        kblk = k_ref[pl.ds(r0, width), :]
        return [lax.dot_general(q_ref[:, mi * d:(mi + 1) * d], kblk[:, mi * d:(mi + 1) * d],
                                (((1,), (1,)), ((), ())), preferred_element_type=F32)
                for mi in range(2)]

    def update(r0, width, s, near):
        nrep = width // _LANES
        vblk = v_ref[pl.ds(r0, width), :]
        if near:
            n = qp_ref[...] - kp_ref[:, pl.ds(r0, width)]
            idx = jnp.clip(n, 0, _LANES - 1)
            table = jnp.broadcast_to(tbl_ref[...], (tq, _LANES))
            bias = jnp.concatenate(
                [jnp.take_along_axis(table, idx[:, c * _LANES:(c + 1) * _LANES], axis=1)
                 for c in range(nrep)], axis=1)
            keep = n >= 0
            s = [jnp.where(keep, sm + bias, _NEG) for sm in s]
            shift_bias = 0.0
        else:
            shift_bias = far_bias
        maps = range(2)
        m_prev = [m_ref[mi] for mi in maps]
        m_new = [jnp.maximum(m_prev[mi], jnp.max(s[mi], axis=-1, keepdims=True) + shift_bias)
                 for mi in maps]
        alpha = [jnp.exp2(m_prev[mi] - m_new[mi]) for mi in maps]
        p = [jnp.exp2(s[mi] - jnp.tile(m_new[mi] - shift_bias, (1, nrep))) for mi in maps]
        for mi in maps:
            psum = p[mi][:, :_LANES]
            for c in range(1, nrep):
                psum = psum + p[mi][:, c * _LANES:(c + 1) * _LANES]
            l_ref[mi] = alpha[mi] * l_ref[mi] + psum
            m_ref[mi] = m_new[mi]
        pv = [jnp.dot(p[mi].astype(BF16), vblk, preferred_element_type=F32) for mi in maps]
        for mi in maps:
            acc_ref[mi] = jnp.tile(alpha[mi], (1, 2 * d // _LANES)) * acc_ref[mi] + pv[mi]

    def far_group(j0, n):
        start = lambda u: pl.multiple_of((j0 + u) * tk, tk)
        s = scores(start(0), tk)
        for u in range(n):
            s_next = scores(start(u + 1), tk) if u + 1 < n else None
            update(start(u), tk, s, False)
            s = s_next

    def group_body(t, carry):
        far_group(group * t, group)
        return carry

    nfar = nfar_ref[b, i]
    ngroup = nfar // group
    lax.fori_loop(0, ngroup, group_body, 0)
    ngrouped = group * ngroup
    for half in (group // 2, group // 4):
        if half >= 1:
            take = (nfar - ngrouped) // half

            @pl.when(take == 1)
            def _(start=ngrouped, half=half):
                far_group(start, half)

            ngrouped = ngrouped + half * take

    def single_body(j, carry):
        r0 = pl.multiple_of(j * tks, tks)
        active = kmin_ref[b, j] <= qmx
        far = (qmn - kmax_ref[b, j]) >= far_dist

        @pl.when(jnp.logical_and(active, far))
        def _():
            update(r0, tks, scores(r0, tks), False)

        @pl.when(jnp.logical_and(active, jnp.logical_not(far)))
        def _():
            update(r0, tks, scores(r0, tks), True)

        return carry

    lax.fori_loop(ngrouped * (tk // tks), last_ref[b, i] + 1, single_body, 0)

    lv = lam_ref[...]
    lam = (jnp.exp(jnp.sum(lv[0:1] * lv[1:2], axis=-1, keepdims=True))
           - jnp.exp(jnp.sum(lv[2:3] * lv[3:4], axis=-1, keepdims=True)) + lam_init)
    l1 = jnp.sum(l_ref[0], axis=-1, keepdims=True)
    l2 = jnp.sum(l_ref[1], axis=-1, keepdims=True)
    o = acc_ref[0] / l1 - lam * (acc_ref[1] / l2)
    o = o * lax.rsqrt(jnp.mean(o * o, axis=-1, keepdims=True) + LN_EPS)
    o_ref[...] = (o * sub_ref[...] * (1.0 - lam_init)).astype(o_ref.dtype)


def _diff_attention(qkv, positions, rel_bias, lam_vecs, subln_w, lam_init, *, tq=512, tk=1024, tks=512):
    B, S, D3 = qkv.shape
    D = D3 // 3
    H = D // (2 * DA_HEAD_DIM)
    w = 2 * DA_HEAD_DIM
    tq = _pick(S, (tq, 256, 128))
    tk = _pick(S, (tk, 256, 128))
    tks = min(tks, tk)
    nq, nk, nks = S // tq, S // tk, S // tks
    buckets = _t5_buckets(_LANES)
    assert buckets[-1] == N_BUCKETS - 1
    far_dist = buckets.index(N_BUCKETS - 1)
    table = rel_bias.astype(F32)[jnp.array(buckets, jnp.int32), :].T * _LOG2E
    table = table.reshape(H, 1, _LANES)

    pq = positions.reshape(B, nq, tq)
    pk = positions.reshape(B, nks, tks)
    qmin, qmax = pq.min(-1), pq.max(-1)
    kmin, kmax = pk.min(-1), pk.max(-1)
    needed = kmin[:, None, :] <= qmax[:, :, None]
    last = jnp.max(jnp.where(needed, jnp.arange(nks, dtype=jnp.int32), 0), axis=-1).astype(jnp.int32)
    kmax_wide = kmax.reshape(B, nk, tk // tks).max(-1)
    far = (qmin[:, :, None] - kmax_wide[:, None, :]) >= far_dist
    nfar = jnp.sum(jnp.cumprod(far.astype(jnp.int32), axis=-1), axis=-1).astype(jnp.int32)
    nfar = jnp.minimum(nfar, (last + 1) // (tk // tks))

    grid_spec = pltpu.PrefetchScalarGridSpec(
        num_scalar_prefetch=6,
        grid=(B, H, nq),
        in_specs=[
            pl.BlockSpec((None, tq, w), lambda b, h, i, *_: (b, i, h)),
            pl.BlockSpec((None, S, w), lambda b, h, i, *_: (b, 0, H + h)),
            pl.BlockSpec((None, S, w), lambda b, h, i, *_: (b, 0, 2 * H + h)),
            pl.BlockSpec((None, tq, 1), lambda b, h, i, *_: (b, i, 0)),
            pl.BlockSpec((None, 1, S), lambda b, h, i, *_: (b, 0, 0)),
            pl.BlockSpec((None, 1, _LANES), lambda b, h, i, *_: (h, 0, 0)),
            pl.BlockSpec((4, DA_HEAD_DIM), lambda b, h, i, *_: (0, 0)),
            pl.BlockSpec((1, w), lambda b, h, i, *_: (0, 0)),
        ],
        out_specs=pl.BlockSpec((None, tq, w), lambda b, h, i, *_: (b, i, h)),
        scratch_shapes=[pltpu.VMEM((2, tq, _LANES), F32), pltpu.VMEM((2, tq, _LANES), F32),
                        pltpu.VMEM((2, tq, w), F32)],
    )
    return pl.pallas_call(
        functools.partial(_attn_kernel, tk=tk, tks=tks, group=max(2, _ATTN_TRIP_KEYS // tk),
                          far_dist=far_dist, lam_init=lam_init),
        out_shape=jax.ShapeDtypeStruct((B, S, D), BF16),
        grid_spec=grid_spec,
        compiler_params=_params(("parallel", "parallel", "arbitrary")),
        name="diff_attention",
    )(qmin, qmax, kmin, kmax, last, nfar,
      qkv, qkv, qkv, positions.reshape(B, S, 1), positions.reshape(B, 1, S),
      table, lam_vecs.astype(F32), subln_w.astype(F32).reshape(1, w))


def _xattn_kernel(x_ref, h_ref, g0_ref, b0_ref, wq_ref, kv_ref, wo_ref, g_ref, b_ref,
                  o_ref, ob_ref, *, alpha):
    dh = CA_HEAD_DIM
    x1 = _layer_norm(alpha * x_ref[...] + h_ref[...], g0_ref[...], b0_ref[...])
    q = jnp.dot(x1.astype(BF16), wq_ref[...], preferred_element_type=F32) * (dh ** -0.5)
    q = q.astype(BF16)
    heads = []
    for hh in range(CA_HEADS):
        kh = kv_ref[:, hh * dh:(hh + 1) * dh]
        vh = kv_ref[:, CA_DIM + hh * dh:CA_DIM + (hh + 1) * dh]
        s = lax.dot_general(q[:, hh * dh:(hh + 1) * dh], kh, (((1,), (1,)), ((), ())),
                            preferred_element_type=F32)
        p = jnp.exp(s - jnp.max(s, axis=-1, keepdims=True))
        p = p / jnp.sum(p, axis=-1, keepdims=True)
        heads.append(jnp.dot(p.astype(BF16), vh, preferred_element_type=F32))
    o = jnp.concatenate(heads, axis=-1).astype(BF16)
    hcat = jnp.dot(o, wo_ref[...], preferred_element_type=F32)
    y = _layer_norm(alpha * x1 + hcat, g_ref[...], b_ref[...])
    o_ref[...] = y
    ob_ref[...] = y.astype(BF16)


def _norm_cross_attention(x, h, g0, b0, kv, w_q, w_o, g, b, alpha, *, B):
    M, D = x.shape
    S = M // B
    NM = kv.shape[1]
    tm = _pick(S, (256, 128))
    nt = S // tm
    row = pl.BlockSpec((tm, D), lambda i: (i, 0))
    vec = pl.BlockSpec((1, D), lambda i: (0, 0))
    return pl.pallas_call(
        functools.partial(_xattn_kernel, alpha=alpha),
        out_shape=(jax.ShapeDtypeStruct((M, D), F32), jax.ShapeDtypeStruct((M, D), BF16)),
        grid=(M // tm,),
        in_specs=[row, row, vec, vec,
                  pl.BlockSpec((D, CA_DIM), lambda i: (0, 0)),
                  pl.BlockSpec((None, NM, 2 * CA_DIM), lambda i: (i // nt, 0, 0)),
                  pl.BlockSpec((CA_DIM, D), lambda i: (0, 0)),
                  vec, vec],
        out_specs=(row, row),
        compiler_params=_params(("parallel",)),
        name="norm_cross_attention",
    )(x, h, g0.reshape(1, D), b0.reshape(1, D), w_q, kv, w_o, g.reshape(1, D), b.reshape(1, D))


def _ffn_up_kernel(x_ref, wg_ref, wv_ref, cwg_ref, cwv_ref, cbg_ref, cbv_ref, o_ref,
                   carry_ref, *, ts, tiles_per_seq):
    i = pl.program_id(0)
    j = pl.program_id(1)
    nsub = x_ref.shape[0] // ts
    row = lax.broadcasted_iota(jnp.int32, (ts, 1), 0)
    wg = wg_ref[...]
    wv = wv_ref[...]

    def up(u):
        x = x_ref[u * ts:(u + 1) * ts, :]
        return (jnp.dot(x, wg, preferred_element_type=F32), jnp.dot(x, wv, preferred_element_type=F32))

    def conv(hcur, prev, cw_ref, cb_ref):
        p1 = prev[7:8, :]
        p2 = prev[6:7, :]
        h1 = jnp.where(row == 0, p1, pltpu.roll(hcur, 1, axis=0))
        h2 = jnp.where(row == 0, p2, jnp.where(row == 1, p1, pltpu.roll(hcur, 2, axis=0)))
        return cw_ref[0:1, :] * h2 + cw_ref[1:2, :] * h1 + cw_ref[2:3, :] * hcur + cb_ref[...]

    prev = (carry_ref[0, j], carry_ref[1, j])
    raw = up(0)
    for u in range(nsub):
        raw_next = up(u + 1) if u + 1 < nsub else None
        seq_start = ((i * nsub + u) % tiles_per_seq) == 0
        gate = conv(raw[0], jnp.where(seq_start, 0.0, prev[0]), cwg_ref, cbg_ref)
        val = conv(raw[1], jnp.where(seq_start, 0.0, prev[1]), cwv_ref, cbv_ref)
        o_ref[u * ts:(u + 1) * ts, :] = (gate * jax.nn.sigmoid(gate) * val).astype(o_ref.dtype)
        prev = (raw[0][ts - 8:, :], raw[1][ts - 8:, :])
        raw = raw_next
    carry_ref[0, j] = prev[0]
    carry_ref[1, j] = prev[1]


def _ffn_up(xb, w_up, conv_w, conv_b, *, S):
    M, D = xb.shape
    F2 = w_up.shape[1]
    F = F2 // 2
    tn = 256
    assert F % tn == 0
    nf = F // tn
    ts = _pick(S, (1024, 512, 256, 128))
    tm = 2 * ts if M % (2 * ts) == 0 else ts
    conv_b = conv_b.reshape(1, F2)
    return pl.pallas_call(
        functools.partial(_ffn_up_kernel, ts=ts, tiles_per_seq=S // ts),
        out_shape=jax.ShapeDtypeStruct((M, F), BF16),
        grid=(M // tm, nf),
        in_specs=[pl.BlockSpec((tm, D), lambda i, j: (i, 0)),
                  pl.BlockSpec((D, tn), lambda i, j: (0, j)),
                  pl.BlockSpec((D, tn), lambda i, j: (0, nf + j)),
                  pl.BlockSpec((CONV_W, tn), lambda i, j: (0, j)),
                  pl.BlockSpec((CONV_W, tn), lambda i, j: (0, nf + j)),
                  pl.BlockSpec((1, tn), lambda i, j: (0, j)),
                  pl.BlockSpec((1, tn), lambda i, j: (0, nf + j))],
        out_specs=pl.BlockSpec((tm, tn), lambda i, j: (i, j)),
        scratch_shapes=[pltpu.VMEM((2, nf, 8, tn), F32)],
        compiler_params=_params(("arbitrary", "arbitrary")),
        name="ffn_up_conv_glu",
    )(xb, w_up, w_up, conv_w, conv_w, conv_b, conv_b)


def _mix_kernel(x_ref, mix_ref, *rest, tiles_per_seq):
    outs, carry_ref = rest[:-1], rest[-1]
    i = pl.program_id(0)
    tm = x_ref.shape[0]
    x = x_ref[...]
    row = lax.broadcasted_iota(jnp.int32, (tm, 1), 0)
    prev_last = jnp.where((i % tiles_per_seq) == 0, 0.0, carry_ref[7:8, :])
    xx = jnp.where(row == 0, prev_last, pltpu.roll(x, 1, axis=0)) - x
    carry_ref[...] = x[tm - 8:, :]
    for m, o_ref in enumerate(outs):
        o_ref[...] = (x + xx * mix_ref[m:m + 1, :]).astype(BF16)


def _token_shift_mix(x, mix, *, S):
    M, D = x.shape
    tm = _pick(S, (256, 128))
    n = mix.shape[0]
    row = pl.BlockSpec((tm, D), lambda i: (i, 0))
    return pl.pallas_call(
        functools.partial(_mix_kernel, tiles_per_seq=S // tm),
        out_shape=tuple(jax.ShapeDtypeStruct((M, D), BF16) for _ in range(n)),
        grid=(M // tm,),
        in_specs=[row, pl.BlockSpec((n, D), lambda i: (0, 0))],
        out_specs=tuple(row for _ in range(n)),
        scratch_shapes=[pltpu.VMEM((8, D), F32)],
        compiler_params=_params(("arbitrary",)),
        name="token_shift_mix",
    )(x, mix)


def _lora_kernel(x_ref, w1_ref, w2_ref, b_ref, o_ref, *, mode):
    t = jnp.dot(x_ref[...], w1_ref[...], preferred_element_type=F32)
    if mode == "decay":
        t = jnp.tanh(t)
    elif mode == "gate":
        t = jax.nn.sigmoid(t)
    z = jnp.dot(t.astype(BF16), w2_ref[...], preferred_element_type=F32) + b_ref[...]
    if mode == "decay":
        u = -z
        sp = jnp.maximum(u, 0.0) + jnp.log(1.0 + jnp.exp(-jnp.abs(u)))
        z = -jnp.exp(-sp - 0.5)
    elif mode == "lr":
        z = jax.nn.sigmoid(z)
    o_ref[...] = z


def _lora(xb, w1, w2, bias, mode):
    M, D = xb.shape
    R = w1.shape[1]
    tm = _pick(M, (512, 256, 128))
    row = pl.BlockSpec((tm, D), lambda i: (i, 0))
    return pl.pallas_call(
        functools.partial(_lora_kernel, mode=mode),
        out_shape=jax.ShapeDtypeStruct((M, D), F32),
        grid=(M // tm,),
        in_specs=[row, pl.BlockSpec((D, R), lambda i: (0, 0)),
                  pl.BlockSpec((R, D), lambda i: (0, 0)),
                  pl.BlockSpec((1, D), lambda i: (0, 0))],
        out_specs=row,
        compiler_params=_params(("parallel",)),
        name="lora_" + mode,
    )(xb, w1, w2, bias.reshape(1, D))


def _dot_bf(a, b):
    return jnp.dot(a.astype(BF16), b.astype(BF16), preferred_element_type=F32)


def _split3(x):
    hi = x.astype(BF16)
    r1 = x - hi.astype(F32)
    mid = r1.astype(BF16)
    lo = (r1 - mid.astype(F32)).astype(BF16)
    return hi, mid, lo


def _dot_exact_rhs(x, w_bf):
    n = x.shape[0]
    r = jnp.dot(jnp.concatenate(_split3(x), axis=0), w_bf, preferred_element_type=F32)
    return r[:n] + r[n:2 * n] + r[2 * n:]


def _dot_exact_lhs(w_bf, x):
    n = x.shape[1]
    r = jnp.dot(w_bf, jnp.concatenate(_split3(x), axis=1), preferred_element_type=F32)
    return r[:, :n] + r[:, n:2 * n] + r[:, 2 * n:]


def _dot_x3(a, b):
    n = a.shape[0]
    a_hi = a.astype(BF16)
    a_lo = (a - a_hi.astype(F32)).astype(BF16)
    b_hi = b.astype(BF16)
    b_lo = (b - b_hi.astype(F32)).astype(BF16)
    r = jnp.dot(jnp.concatenate([a_hi, a_lo], axis=0), b_hi, preferred_element_type=F32)
    return r[:n] + r[n:] + jnp.dot(a_hi, b_lo, preferred_element_type=F32)


def _head_ones():
    r = lax.broadcasted_iota(jnp.int32, (_LANES, _LANES), 0) // RW_HEAD
    c = lax.broadcasted_iota(jnp.int32, (_LANES, _LANES), 1) // RW_HEAD
    return jnp.where(r == c, 1.0, 0.0).astype(BF16)


def _rwkv_prep_kernel(r_ref, k_ref, v_ref, lw_ref, a_ref, kk_ref, ka_ref, rk_ref, lb_ref,
                      rp_ref, y0_ref, e_ref, g_ref, h_ref, *, npairs):
    L = RW_CHUNK
    n2 = 2 * L
    pairs = range(npairs)
    sls = [slice(p * _LANES, (p + 1) * _LANES) for p in pairs]
    ones_bd = _head_ones()
    ri = lax.broadcasted_iota(jnp.int32, (n2, n2), 0)
    ci = lax.broadcasted_iota(jnp.int32, (n2, n2), 1)
    strict = ci < ri
    incl = ci <= ri
    diag = ri == ci
    tri = jnp.where(lax.broadcasted_iota(jnp.int32, (L, L), 1)
                    <= lax.broadcasted_iota(jnp.int32, (L, L), 0), 1.0, 0.0).astype(BF16)
    head0 = lax.broadcasted_iota(jnp.int32, (L, _LANES), 1) < RW_HEAD

    def stack(x):
        return jnp.concatenate([jnp.where(head0, x, 0.0), jnp.where(head0, 0.0, x)], axis=0)

    def unstack(x):
        return x[:L] + x[L:]

    r = [r_ref[:, sl] for sl in sls]
    k = [k_ref[:, sl] for sl in sls]
    v = [v_ref[:, sl] for sl in sls]
    lw = [lw_ref[:, sl] for sl in sls]
    a = [a_ref[:, sl] for sl in sls]
    kk = [k[p] * kk_ref[:, sls[p]] for p in pairs]
    kmod = [k[p] * (1.0 + (a[p] - 1.0) * ka_ref[:, sls[p]]) for p in pairs]
    cl = [_dot_exact_lhs(tri, lw[p]) for p in pairs]
    hs = [_dot_exact_rhs(jnp.concatenate([kk[p] * kk[p], r[p] * kmod[p] * rk_ref[:, sls[p]]], axis=0),
                         ones_bd) for p in pairs]
    for p in pairs:
        e_ref[:, sls[p]] = lb_ref[:, sls[p]] + hs[p][L:] * v[p]
    kk = [kk[p] / jnp.maximum(jnp.sqrt(hs[p][:L]), 1e-12) for p in pairs]
    bv = [kk[p] * a[p] for p in pairs]
    cl_end = [c[L - 1:L, :] for c in cl]
    c_inv = [jnp.exp(-c) for c in cl]
    c_rem = [jnp.exp(cl_end[p] - cl[p]) for p in pairs]
    a_s = [stack(-kk[p] * jnp.exp(cl[p] - lw[p])) for p in pairs]
    r_s = [stack(r[p] * jnp.exp(cl[p])) for p in pairs]
    v_s = [stack(v[p]).astype(BF16) for p in pairs]
    bk_t = [jnp.concatenate([stack(bv[p] * c_inv[p]), stack(kmod[p] * c_inv[p])], axis=0)
            .astype(BF16).T for p in pairs]
    ar_bk = [_dot_bf(jnp.concatenate([a_s[p], r_s[p]], axis=0), bk_t[p]) for p in pairs]
    nab = [jnp.where(strict, m[:n2, :n2], 0.0) for m in ar_bk]
    nak = [jnp.where(strict, m[:n2, n2:], 0.0) for m in ar_bk]
    mrb = [jnp.where(incl, m[n2:, :n2], 0.0) for m in ar_bk]
    mrk = [jnp.where(incl, m[n2:, n2:], 0.0) for m in ar_bk]
    xv = [_dot_bf(jnp.concatenate([nak[p], mrk[p], stack(kmod[p] * c_rem[p]).T], axis=0), v_s[p])
          for p in pairs]
    levels = int(math.log2(L)) - 1
    t_inv = [jnp.where(diag, 1.0, m) for m in nab]
    x = [_dot_bf(m, m) for m in nab]
    for lvl in range(levels):
        if lvl < levels - 1:
            tx = [_dot_bf(jnp.concatenate([t_inv[p], x[p]], axis=0), x[p]) for p in pairs]
            t_inv = [t_inv[p] + tx[p][:n2] for p in pairs]
            x = [tx[p][n2:] for p in pairs]
        else:
            t_inv = [t_inv[p] + _dot_bf(t_inv[p], x[p]) for p in pairs]
    pq = [_dot_bf(t_inv[p], jnp.concatenate([a_s[p], xv[p][:n2]], axis=1)) for p in pairs]
    rg = [_dot_bf(jnp.concatenate([mrb[p], stack(bv[p] * c_rem[p]).T], axis=0), pq[p]) for p in pairs]
    for p in pairs:
        rp_ref[:, sls[p]] = unstack(r_s[p] + rg[p][:n2, :_LANES])
        y0_ref[:, sls[p]] = unstack(rg[p][:n2, _LANES:] + xv[p][n2:2 * n2])
        g_ref[p] = jnp.where(diag, jnp.exp(cl_end[p]), 0.0) + rg[p][n2:, :_LANES]
        h_ref[p] = rg[p][n2:, _LANES:] + xv[p][2 * n2:]


def _rwkv_scan_kernel(rp_ref, y0_ref, e_ref, gate_ref, g_ref, h_ref, lnw_ref, o_ref, st_ref, *, npairs):
    c = pl.program_id(2)

    @pl.when(c == 0)
    def _():
        st_ref[...] = jnp.zeros(st_ref.shape, F32)

    ones_bd = _head_ones()
    inv_n = 1.0 / RW_HEAD
    pairs = range(npairs)
    sls = [slice(p * _LANES, (p + 1) * _LANES) for p in pairs]
    prod = [_dot_x3(jnp.concatenate([g_ref[p], rp_ref[:, sls[p]]], axis=0), st_ref[p]) for p in pairs]
    for p in pairs:
        st_ref[p] = prod[p][:_LANES] + h_ref[p]
    y = [prod[p][_LANES:] + y0_ref[:, sls[p]] for p in pairs]
    mu = [_dot_exact_rhs(y[p], ones_bd) * inv_n for p in pairs]
    yc = [y[p] - mu[p] for p in pairs]
    var = [_dot_exact_rhs(yc[p] * yc[p], ones_bd) * inv_n for p in pairs]
    for p in pairs:
        yn = yc[p] * lax.rsqrt(var[p] + GN_EPS)
        o_ref[:, sls[p]] = ((yn * lnw_ref[:, sls[p]] + e_ref[:, sls[p]])
                            * gate_ref[:, sls[p]]).astype(o_ref.dtype)


def _rwkv_core(r, k, v, lw, a, gate, k_k, k_a, r_k, lnx_w, lnx_b, *, B):
    M, D = r.shape
    S = M // B
    L = RW_CHUNK
    nc = S // L
    ngroups = D // _LANES
    np1 = _pick(ngroups, (8, 4, 2, 1))
    np2 = _pick(ngroups, (16, 8, 4, 2, 1))
    vec = lambda t: t.astype(F32).reshape(1, D)

    w1 = np1 * _LANES
    tile1 = pl.BlockSpec((L, w1), lambda b, c, g: (b * nc + c, g))
    par1 = pl.BlockSpec((1, w1), lambda b, c, g: (0, g))
    mat1 = pl.BlockSpec((None, None, np1, _LANES, _LANES), lambda b, c, g: (b, c, g, 0, 0))
    md = jax.ShapeDtypeStruct((M, D), F32)
    gh = jax.ShapeDtypeStruct((B, nc, ngroups, _LANES, _LANES), F32)
    rp, y0, e, gm, hm = pl.pallas_call(
        functools.partial(_rwkv_prep_kernel, npairs=np1),
        out_shape=(md, md, md, gh, gh),
        grid=(B, nc, ngroups // np1),
        in_specs=[tile1] * 5 + [par1] * 4,
        out_specs=(tile1, tile1, tile1, mat1, mat1),
        compiler_params=_params(("parallel", "parallel", "parallel")),
        name="rwkv_chunk_prep",
    )(r, k, v, lw, a, vec(k_k), vec(k_a), vec(r_k), vec(lnx_b))

    w2 = np2 * _LANES
    tile2 = pl.BlockSpec((L, w2), lambda b, g, c: (b * nc + c, g))
    par2 = pl.BlockSpec((1, w2), lambda b, g, c: (0, g))
    mat2 = pl.BlockSpec((None, None, np2, _LANES, _LANES), lambda b, g, c: (b, c, g, 0, 0))
    return pl.pallas_call(
        functools.partial(_rwkv_scan_kernel, npairs=np2),
        out_shape=jax.ShapeDtypeStruct((M, D), BF16),
        grid=(B, ngroups // np2, nc),
        in_specs=[tile2] * 4 + [mat2, mat2, par2],
        out_specs=tile2,
        scratch_shapes=[pltpu.VMEM((np2, _LANES, _LANES), F32)],
        compiler_params=_params(("parallel", "parallel", "arbitrary")),
        name="rwkv_state_scan",
    )(rp, y0, e, gate, gm, hm, vec(lnx_w))


def kernel(x, mem, positions, rel_bias, da_w_qkv, da_lam, da_subln, da_w_o, rw_mix, rw_w_rkv, rw_w0, rw_w1, rw_w2, rw_a0, rw_a1, rw_a2, rw_g1, rw_g2, rw_k_k, rw_k_a, rw_r_k, rw_lnx_w, rw_lnx_b, rw_w_o, ca_w_q, ca_w_kv, ca_w_o, ffn_w_up, ffn_conv_w, ffn_conv_b, ffn_w_down, ln_g, ln_b):
    B, S, D = x.shape
    M = B * S
    depth = ln_g.shape[0]
    alpha = (2 * depth) ** 0.25
    bf = lambda t: t.astype(BF16)
    memb = bf(mem).reshape(B * mem.shape[1], D)
    F = ffn_w_down.shape[1]
    tk_down = F // 2 if (F // 2) % _LANES == 0 else F

    xf = x.reshape(M, D)
    xb = bf(xf)
    for i in range(depth):
        j = i // 2
        if i % 2 == 0:
            lam_init = 0.8 - 0.6 * math.exp(-0.3 * i)
            qkv = _matmul(xb, bf(da_w_qkv[j]), BF16, scaled_cols=D,
                          col_scale=DA_HEAD_DIM ** -0.5 * _LOG2E)
            att = _diff_attention(qkv.reshape(B, S, 3 * D), positions, rel_bias, da_lam[j],
                                  da_subln[j], lam_init)
            h = _matmul(att.reshape(M, D), bf(da_w_o[j]), F32)
        else:
            xr, xw, xk, xv, xa, xg = _token_shift_mix(xf, rw_mix[j].astype(F32), S=S)
            r = _matmul(xr, bf(rw_w_rkv[j, 0]), F32)
            k = _matmul(xk, bf(rw_w_rkv[j, 1]), F32)
            v = _matmul(xv, bf(rw_w_rkv[j, 2]), F32)
            lw = _lora(xw, bf(rw_w1[j]), bf(rw_w2[j]), rw_w0[j].astype(F32), "decay")
            a = _lora(xa, bf(rw_a1[j]), bf(rw_a2[j]), rw_a0[j].astype(F32), "lr")
            g = _lora(xg, bf(rw_g1[j]), bf(rw_g2[j]), jnp.zeros((D,), F32), "gate")
            o = _rwkv_core(r, k, v, lw, a, g, rw_k_k[j], rw_k_a[j], rw_r_k[j].reshape(D),
                           rw_lnx_w[j], rw_lnx_b[j], B=B)
            h = _matmul(o, bf(rw_w_o[j]), F32)

        kv = _matmul(memb, bf(ca_w_kv[i]), BF16).reshape(B, mem.shape[1], 2 * CA_DIM)
        xf, xb = _norm_cross_attention(xf, h, ln_g[i, 0], ln_b[i, 0], kv, bf(ca_w_q[i]),
                                       bf(ca_w_o[i]), ln_g[i, 1], ln_b[i, 1], alpha, B=B)

        act = _ffn_up(xb, bf(ffn_w_up[i]), ffn_conv_w[i].astype(F32), ffn_conv_b[i].astype(F32), S=S)
        h = _matmul(act, bf(ffn_w_down[i]), F32, tn=512, tk=tk_down)
        xf, xb = _deepnorm(xf, h, ln_g[i, 2], ln_b[i, 2], alpha)
    return xf.reshape(B, S, D)
```

```python
import functools
import math

import jax
import jax.numpy as jnp
from jax import lax
from jax.experimental import pallas as pl
from jax.experimental.pallas import tpu as pltpu

F32 = jnp.float32
BF16 = jnp.bfloat16

_VMEM_LIMIT_BYTES = 56 * 1024 * 1024
_LANES = 128

DA_HEAD_DIM = 128
N_BUCKETS = 32
MAX_DIST = 128
RW_HEAD = 64
GN_EPS = 64e-5
CA_HEADS = 4
CA_HEAD_DIM = 128
CA_DIM = CA_HEADS * CA_HEAD_DIM
CONV_W = 3
LN_EPS = 1e-5
RW_CHUNK = 64

_ATTN_TRIP_KEYS = 2048
_NEG = float(jnp.finfo(jnp.float32).min)
_LOG2E = math.log2(math.e)


def _params(sem):
    return pltpu.CompilerParams(dimension_semantics=sem, vmem_limit_bytes=_VMEM_LIMIT_BYTES)


def _pick(n, prefs):
    for p in prefs:
        if n % p == 0:
            return p
    return n


def _mm_kernel(x_ref, w_ref, o_ref, *scratch, nk, scaled_tiles, col_scale):
    part = jnp.dot(x_ref[...], w_ref[...], preferred_element_type=F32)

    def finish(acc):
        if scaled_tiles:
            acc = acc * jnp.where(pl.program_id(1) < scaled_tiles, col_scale, 1.0)
        o_ref[...] = acc.astype(o_ref.dtype)

    if nk == 1:
        finish(part)
        return
    (acc_ref,) = scratch
    k = pl.program_id(2)

    @pl.when(k == 0)
    def _():
        acc_ref[...] = part

    @pl.when(jnp.logical_and(k > 0, k < nk - 1))
    def _():
        acc_ref[...] += part

    @pl.when(k == nk - 1)
    def _():
        finish(acc_ref[...] + part)


def _matmul(x, w, out_dtype, *, tm=1024, tn=1024, tk=None, scaled_cols=0, col_scale=1.0):
    M, K = x.shape
    N = w.shape[1]
    tm = _pick(M, (tm, 512, 256, 128))
    tn = _pick(N, (tn, 512, 256, 128))
    tk = K if tk is None else tk
    nk = K // tk
    assert K % tk == 0 and scaled_cols % tn == 0
    scratch = [pltpu.VMEM((tm, tn), F32)] if nk > 1 else []
    return pl.pallas_call(
        functools.partial(_mm_kernel, nk=nk, scaled_tiles=scaled_cols // tn, col_scale=col_scale),
        out_shape=jax.ShapeDtypeStruct((M, N), out_dtype),
        grid=(M // tm, N // tn, nk),
        in_specs=[pl.BlockSpec((tm, tk), lambda i, j, k: (i, k)),
                  pl.BlockSpec((tk, tn), lambda i, j, k: (k, j))],
        out_specs=pl.BlockSpec((tm, tn), lambda i, j, k: (i, j)),
        scratch_shapes=scratch,
        compiler_params=_params(("parallel", "parallel", "arbitrary")),
        name="matmul",
    )(x, w)


def _layer_norm(z, g, b):
    mu = jnp.mean(z, axis=-1, keepdims=True)
    zc = z - mu
    var = jnp.mean(zc * zc, axis=-1, keepdims=True)
    return zc * lax.rsqrt(var + LN_EPS) * g + b


def _ln_kernel(x_ref, h_ref, g_ref, b_ref, o_ref, *bf16_out, alpha):
    y = _layer_norm(alpha * x_ref[...] + h_ref[...], g_ref[...], b_ref[...])
    o_ref[...] = y
    for ob_ref in bf16_out:
        ob_ref[...] = y.astype(BF16)


def _deepnorm(x, h, g, b, alpha, *, with_bf16):
    M, D = x.shape
    tm = _pick(M, (256, 128))
    row = pl.BlockSpec((tm, D), lambda i: (i, 0))
    vec = pl.BlockSpec((1, D), lambda i: (0, 0))
    shapes = (jax.ShapeDtypeStruct((M, D), F32),) + ((jax.ShapeDtypeStruct((M, D), BF16),) if with_bf16 else ())
    return pl.pallas_call(
        functools.partial(_ln_kernel, alpha=alpha),
        out_shape=shapes,
        grid=(M // tm,),
        in_specs=[row, row, vec, vec],
        out_specs=tuple(row for _ in shapes),
        compiler_params=_params(("parallel",)),
        name="deepnorm",
    )(x, h, g.reshape(1, D), b.reshape(1, D))


def _t5_buckets(n_dist):
    max_exact = N_BUCKETS // 2
    out = []
    for n in range(n_dist):
        if n < max_exact:
            out.append(n)
            continue
        val = math.log(n / max_exact) / math.log(MAX_DIST / max_exact) * (N_BUCKETS - max_exact)
        assert n == max_exact or n >= MAX_DIST or abs(val - round(val)) > 1e-3
        out.append(min(max_exact + int(val), N_BUCKETS - 1))
    return out


def _attn_kernel(qmin_ref, qmax_ref, kmin_ref, kmax_ref, last_ref, nfar_ref,
                 q_ref, k_ref, v_ref, qp_ref, kp_ref, tbl_ref, lam_ref, sub_ref,
                 o_ref, m_ref, l_ref, acc_ref, *, tk, tks, group, far_dist, lam_init):
    b = pl.program_id(0)
    i = pl.program_id(2)
    d = DA_HEAD_DIM
    tq = q_ref.shape[0]
    qmn = qmin_ref[b, i]
    qmx = qmax_ref[b, i]
    far_bias = tbl_ref[:, _LANES - 1:_LANES]

    m_ref[...] = jnp.full(m_ref.shape, _NEG, F32)
    l_ref[...] = jnp.zeros(l_ref.shape, F32)
    acc_ref[...] = jnp.zeros(acc_ref.shape, F32)

    def scores(r0, width):
        kblk = k_ref[pl.ds(r0, width), :]
        return [lax.dot_general(q_ref[:, mi * d:(mi + 1) * d], kblk[:, mi * d:(mi + 1) * d],
                                (((1,), (1,)), ((), ())), preferred_element_type=F32)
                for mi in range(2)]

    def update(r0, width, s, near):
        nrep = width // _LANES
        vblk = v_ref[pl.ds(r0, width), :]
        if near:
            n = qp_ref[...] - kp_ref[:, pl.ds(r0, width)]
            idx = jnp.clip(n, 0, _LANES - 1)
            table = jnp.broadcast_to(tbl_ref[...], (tq, _LANES))
            bias = jnp.concatenate(
                [jnp.take_along_axis(table, idx[:, c * _LANES:(c + 1) * _LANES], axis=1)
                 for c in range(nrep)], axis=1)
            keep = n >= 0
            s = [jnp.where(keep, sm + bias, _NEG) for sm in s]
            shift_bias = 0.0
        else:
            shift_bias = far_bias
        maps = range(2)
        m_prev = [m_ref[mi] for mi in maps]
        m_new = [jnp.maximum(m_prev[mi], jnp.max(s[mi], axis=-1, keepdims=True) + shift_bias)
                 for mi in maps]
        alpha = [jnp.exp2(m_prev[mi] - m_new[mi]) for mi in maps]
        p = [jnp.exp2(s[mi] - jnp.tile(m_new[mi] - shift_bias, (1, nrep))) for mi in maps]
        for mi in maps:
            psum = p[mi][:, :_LANES]
            for c in range(1, nrep):
                psum = psum + p[mi][:, c * _LANES:(c + 1) * _LANES]
            l_ref[mi] = alpha[mi] * l_ref[mi] + psum
            m_ref[mi] = m_new[mi]
        pv = [jnp.dot(p[mi].astype(BF16), vblk, preferred_element_type=F32) for mi in maps]
        for mi in maps:
            acc_ref[mi] = jnp.tile(alpha[mi], (1, 2 * d // _LANES)) * acc_ref[mi] + pv[mi]

    def far_group(j0, n):
        start = lambda u: pl.multiple_of((j0 + u) * tk, tk)
        s = scores(start(0), tk)
        for u in range(n):
            s_next = scores(start(u + 1), tk) if u + 1 < n else None
            update(start(u), tk, s, False)
            s = s_next

    def group_body(t, carry):
        far_group(group * t, group)
        return carry

    nfar = nfar_ref[b, i]
    ngroup = nfar // group
    lax.fori_loop(0, ngroup, group_body, 0)
    ngrouped = group * ngroup
    for half in (group // 2, group // 4):
        if half >= 1:
            take = (nfar - ngrouped) // half

            @pl.when(take == 1)
            def _(start=ngrouped, half=half):
                far_group(start, half)

            ngrouped = ngrouped + half * take

    def single_body(j, carry):
        r0 = pl.multiple_of(j * tks, tks)
        active = kmin_ref[b, j] <= qmx
        far = (qmn - kmax_ref[b, j]) >= far_dist

        @pl.when(jnp.logical_and(active, far))
        def _():
            update(r0, tks, scores(r0, tks), False)

        @pl.when(jnp.logical_and(active, jnp.logical_not(far)))
        def _():
            update(r0, tks, scores(r0, tks), True)

        return carry

    first = ngrouped * (tk // tks)
    stop = last_ref[b, i] + 1
    paired = (stop - first) >= 2
    single_stop = jnp.where(paired, stop - 2, stop)
    lax.fori_loop(first, single_stop, single_body, 0)

    @pl.when(paired)
    def _():
        r_a = pl.multiple_of((stop - 2) * tks, tks)
        r_b = pl.multiple_of((stop - 1) * tks, tks)
        s_a = scores(r_a, tks)
        s_b = scores(r_b, tks)
        update(r_a, tks, s_a, True)
        update(r_b, tks, s_b, True)

    lv = lam_ref[...]
    lam = (jnp.exp(jnp.sum(lv[0:1] * lv[1:2], axis=-1, keepdims=True))
           - jnp.exp(jnp.sum(lv[2:3] * lv[3:4], axis=-1, keepdims=True)) + lam_init)
    l1 = jnp.sum(l_ref[0], axis=-1, keepdims=True)
    l2 = jnp.sum(l_ref[1], axis=-1, keepdims=True)
    o = acc_ref[0] / l1 - lam * (acc_ref[1] / l2)
    o = o * lax.rsqrt(jnp.mean(o * o, axis=-1, keepdims=True) + LN_EPS)
    o_ref[...] = (o * sub_ref[...] * (1.0 - lam_init)).astype(o_ref.dtype)


def _diff_attention(qkv, positions, rel_bias, lam_vecs, subln_w, lam_init, *, tq=512, tk=1024, tks=512):
    B, S, D3 = qkv.shape
    D = D3 // 3
    H = D // (2 * DA_HEAD_DIM)
    w = 2 * DA_HEAD_DIM
    tq = _pick(S, (tq, 256, 128))
    tk = _pick(S, (tk, 256, 128))
    tks = min(tks, tk)
    nq, nk, nks = S // tq, S // tk, S // tks
    buckets = _t5_buckets(_LANES)
    assert buckets[-1] == N_BUCKETS - 1
    far_dist = buckets.index(N_BUCKETS - 1)
    table = rel_bias.astype(F32)[jnp.array(buckets, jnp.int32), :].T * _LOG2E
    table = table.reshape(H, 1, _LANES)

    pq = positions.reshape(B, nq, tq)
    pk = positions.reshape(B, nks, tks)
    qmin, qmax = pq.min(-1), pq.max(-1)
    kmin, kmax = pk.min(-1), pk.max(-1)
    needed = kmin[:, None, :] <= qmax[:, :, None]
    last = jnp.max(jnp.where(needed, jnp.arange(nks, dtype=jnp.int32), 0), axis=-1).astype(jnp.int32)
    kmax_wide = kmax.reshape(B, nk, tk // tks).max(-1)
    far = (qmin[:, :, None] - kmax_wide[:, None, :]) >= far_dist
    nfar = jnp.sum(jnp.cumprod(far.astype(jnp.int32), axis=-1), axis=-1).astype(jnp.int32)
    nfar = jnp.minimum(nfar, (last + 1) // (tk // tks))

    grid_spec = pltpu.PrefetchScalarGridSpec(
        num_scalar_prefetch=6,
        grid=(B, H, nq),
        in_specs=[
            pl.BlockSpec((None, tq, w), lambda b, h, i, *_: (b, i, h)),
            pl.BlockSpec((None, S, w), lambda b, h, i, *_: (b, 0, H + h)),
            pl.BlockSpec((None, S, w), lambda b, h, i, *_: (b, 0, 2 * H + h)),
            pl.BlockSpec((None, tq, 1), lambda b, h, i, *_: (b, i, 0)),
            pl.BlockSpec((None, 1, S), lambda b, h, i, *_: (b, 0, 0)),
            pl.BlockSpec((None, 1, _LANES), lambda b, h, i, *_: (h, 0, 0)),
            pl.BlockSpec((4, DA_HEAD_DIM), lambda b, h, i, *_: (0, 0)),
            pl.BlockSpec((1, w), lambda b, h, i, *_: (0, 0)),
        ],
        out_specs=pl.BlockSpec((None, tq, w), lambda b, h, i, *_: (b, i, h)),
        scratch_shapes=[pltpu.VMEM((2, tq, _LANES), F32), pltpu.VMEM((2, tq, _LANES), F32),
                        pltpu.VMEM((2, tq, w), F32)],
    )
    return pl.pallas_call(
        functools.partial(_attn_kernel, tk=tk, tks=tks, group=max(2, _ATTN_TRIP_KEYS // tk),
                          far_dist=far_dist, lam_init=lam_init),
        out_shape=jax.ShapeDtypeStruct((B, S, D), BF16),
        grid_spec=grid_spec,
        compiler_params=_params(("parallel", "parallel", "arbitrary")),
        name="diff_attention",
    )(qmin, qmax, kmin, kmax, last, nfar,
      qkv, qkv, qkv, positions.reshape(B, S, 1), positions.reshape(B, 1, S),
      table, lam_vecs.astype(F32), subln_w.astype(F32).reshape(1, w))


def _xattn_kernel(x_ref, h_ref, g0_ref, b0_ref, wq_ref, kv_ref, wo_ref, g_ref, b_ref,
                  o_ref, ob_ref, *, alpha):
    dh = CA_HEAD_DIM
    x1 = _layer_norm(alpha * x_ref[...] + h_ref[...], g0_ref[...], b0_ref[...])
    q = jnp.dot(x1.astype(BF16), wq_ref[...], preferred_element_type=F32) * (dh ** -0.5)
    q = q.astype(BF16)
    heads = []
    for hh in range(CA_HEADS):
        kh = kv_ref[:, hh * dh:(hh + 1) * dh]
        vh = kv_ref[:, CA_DIM + hh * dh:CA_DIM + (hh + 1) * dh]
        s = lax.dot_general(q[:, hh * dh:(hh + 1) * dh], kh, (((1,), (1,)), ((), ())),
                            preferred_element_type=F32)
        p = jnp.exp(s - jnp.max(s, axis=-1, keepdims=True))
        p = p / jnp.sum(p, axis=-1, keepdims=True)
        heads.append(jnp.dot(p.astype(BF16), vh, preferred_element_type=F32))
    o = jnp.concatenate(heads, axis=-1).astype(BF16)
    hcat = jnp.dot(o, wo_ref[...], preferred_element_type=F32)
    y = _layer_norm(alpha * x1 + hcat, g_ref[...], b_ref[...])
    o_ref[...] = y
    ob_ref[...] = y.astype(BF16)


def _norm_cross_attention(x, h, g0, b0, kv, w_q, w_o, g, b, alpha, *, B):
    M, D = x.shape
    S = M // B
    NM = kv.shape[1]
    tm = _pick(S, (256, 128))
    nt = S // tm
    row = pl.BlockSpec((tm, D), lambda i: (i, 0))
    vec = pl.BlockSpec((1, D), lambda i: (0, 0))
    return pl.pallas_call(
        functools.partial(_xattn_kernel, alpha=alpha),
        out_shape=(jax.ShapeDtypeStruct((M, D), F32), jax.ShapeDtypeStruct((M, D), BF16)),
        grid=(M // tm,),
        in_specs=[row, row, vec, vec,
                  pl.BlockSpec((D, CA_DIM), lambda i: (0, 0)),
                  pl.BlockSpec((None, NM, 2 * CA_DIM), lambda i: (i // nt, 0, 0)),
                  pl.BlockSpec((CA_DIM, D), lambda i: (0, 0)),
                  vec, vec],
        out_specs=(row, row),
        compiler_params=_params(("parallel",)),
        name="norm_cross_attention",
    )(x, h, g0.reshape(1, D), b0.reshape(1, D), w_q, kv, w_o, g.reshape(1, D), b.reshape(1, D))


def _ffn_up_kernel(x_ref, wg_ref, wv_ref, cwg_ref, cwv_ref, cbg_ref, cbv_ref, o_ref,
                   carry_ref, *, ts, tiles_per_seq):
    i = pl.program_id(0)
    j = pl.program_id(1)
    nsub = x_ref.shape[0] // ts
    row = lax.broadcasted_iota(jnp.int32, (ts, 1), 0)
    wg = wg_ref[...]
    wv = wv_ref[...]

    def up(u):
        x = x_ref[u * ts:(u + 1) * ts, :]
        return (jnp.dot(x, wg, preferred_element_type=F32), jnp.dot(x, wv, preferred_element_type=F32))

    def conv(hcur, prev, cw_ref, cb_ref):
        p1 = prev[7:8, :]
        p2 = prev[6:7, :]
        h1 = jnp.where(row == 0, p1, pltpu.roll(hcur, 1, axis=0))
        h2 = jnp.where(row == 0, p2, jnp.where(row == 1, p1, pltpu.roll(hcur, 2, axis=0)))
        return cw_ref[0:1, :] * h2 + cw_ref[1:2, :] * h1 + cw_ref[2:3, :] * hcur + cb_ref[...]

    prev = (carry_ref[0, j], carry_ref[1, j])
    raw = up(0)
    for u in range(nsub):
        raw_next = up(u + 1) if u + 1 < nsub else None
        seq_start = ((i * nsub + u) % tiles_per_seq) == 0
        gate = conv(raw[0], jnp.where(seq_start, 0.0, prev[0]), cwg_ref, cbg_ref)
        val = conv(raw[1], jnp.where(seq_start, 0.0, prev[1]), cwv_ref, cbv_ref)
        o_ref[u * ts:(u + 1) * ts, :] = (gate * jax.nn.sigmoid(gate) * val).astype(o_ref.dtype)
        prev = (raw[0][ts - 8:, :], raw[1][ts - 8:, :])
        raw = raw_next
    carry_ref[0, j] = prev[0]
    carry_ref[1, j] = prev[1]


def _ffn_up(xb, w_up, conv_w, conv_b, *, S):
    M, D = xb.shape
    F2 = w_up.shape[1]
    F = F2 // 2
    tn = 256
    assert F % tn == 0
    nf = F // tn
    ts = _pick(S, (1024, 512, 256, 128))
    tm = 2 * ts if M % (2 * ts) == 0 else ts
    conv_b = conv_b.reshape(1, F2)
    return pl.pallas_call(
        functools.partial(_ffn_up_kernel, ts=ts, tiles_per_seq=S // ts),
        out_shape=jax.ShapeDtypeStruct((M, F), BF16),
        grid=(M // tm, nf),
        in_specs=[pl.BlockSpec((tm, D), lambda i, j: (i, 0)),
                  pl.BlockSpec((D, tn), lambda i, j: (0, j)),
                  pl.BlockSpec((D, tn), lambda i, j: (0, nf + j)),
                  pl.BlockSpec((CONV_W, tn), lambda i, j: (0, j)),
                  pl.BlockSpec((CONV_W, tn), lambda i, j: (0, nf + j)),
                  pl.BlockSpec((1, tn), lambda i, j: (0, j)),
                  pl.BlockSpec((1, tn), lambda i, j: (0, nf + j))],
        out_specs=pl.BlockSpec((tm, tn), lambda i, j: (i, j)),
        scratch_shapes=[pltpu.VMEM((2, nf, 8, tn), F32)],
        compiler_params=_params(("arbitrary", "arbitrary")),
        name="ffn_up_conv_glu",
    )(xb, w_up, w_up, conv_w, conv_w, conv_b, conv_b)


def _mix_kernel(x_ref, h_ref, g_ref, b_ref, mix_ref, xo_ref, *rest, alpha, tiles_per_seq):
    outs, carry_ref = rest[:-1], rest[-1]
    i = pl.program_id(0)
    tm = x_ref.shape[0]
    x = _layer_norm(alpha * x_ref[...] + h_ref[...], g_ref[...], b_ref[...])
    xo_ref[...] = x
    row = lax.broadcasted_iota(jnp.int32, (tm, 1), 0)
    prev_last = jnp.where((i % tiles_per_seq) == 0, 0.0, carry_ref[7:8, :])
    xx = jnp.where(row == 0, prev_last, pltpu.roll(x, 1, axis=0)) - x
    carry_ref[...] = x[tm - 8:, :]
    for m, o_ref in enumerate(outs):
        o_ref[...] = (x + xx * mix_ref[m:m + 1, :]).astype(BF16)


def _norm_token_shift_mix(x, h, g, b, mix, alpha, *, S):
    M, D = x.shape
    tm = _pick(S, (128,))
    n = mix.shape[0]
    row = pl.BlockSpec((tm, D), lambda i: (i, 0))
    vec = pl.BlockSpec((1, D), lambda i: (0, 0))
    outs = pl.pallas_call(
        functools.partial(_mix_kernel, alpha=alpha, tiles_per_seq=S // tm),
        out_shape=(jax.ShapeDtypeStruct((M, D), F32),)
        + tuple(jax.ShapeDtypeStruct((M, D), BF16) for _ in range(n)),
        grid=(M // tm,),
        in_specs=[row, row, vec, vec, pl.BlockSpec((n, D), lambda i: (0, 0))],
        out_specs=(row,) + tuple(row for _ in range(n)),
        scratch_shapes=[pltpu.VMEM((8, D), F32)],
        compiler_params=_params(("arbitrary",)),
        name="norm_token_shift_mix",
    )(x, h, g.reshape(1, D), b.reshape(1, D), mix)
    return outs[0], outs[1:]


def _lora_kernel(x_ref, w1_ref, w2_ref, b_ref, o_ref, *, mode):
    t = jnp.dot(x_ref[...], w1_ref[...], preferred_element_type=F32)
    if mode == "decay":
        t = jnp.tanh(t)
    elif mode == "gate":
        t = jax.nn.sigmoid(t)
    z = jnp.dot(t.astype(BF16), w2_ref[...], preferred_element_type=F32) + b_ref[...]
    if mode == "decay":
        u = -z
        sp = jnp.maximum(u, 0.0) + jnp.log(1.0 + jnp.exp(-jnp.abs(u)))
        z = -jnp.exp(-sp - 0.5)
    elif mode == "lr":
        z = jax.nn.sigmoid(z)
    o_ref[...] = z


def _lora(xb, w1, w2, bias, mode):
    M, D = xb.shape
    R = w1.shape[1]
    tm = _pick(M, (512, 256, 128))
    row = pl.BlockSpec((tm, D), lambda i: (i, 0))
    return pl.pallas_call(
        functools.partial(_lora_kernel, mode=mode),
        out_shape=jax.ShapeDtypeStruct((M, D), F32),
        grid=(M // tm,),
        in_specs=[row, pl.BlockSpec((D, R), lambda i: (0, 0)),
                  pl.BlockSpec((R, D), lambda i: (0, 0)),
                  pl.BlockSpec((1, D), lambda i: (0, 0))],
        out_specs=row,
        compiler_params=_params(("parallel",)),
        name="lora_" + mode,
    )(xb, w1, w2, bias.reshape(1, D))


def _dot_bf(a, b):
    return jnp.dot(a.astype(BF16), b.astype(BF16), preferred_element_type=F32)


def _split3(x):
    hi = x.astype(BF16)
    r1 = x - hi.astype(F32)
    mid = r1.astype(BF16)
    lo = (r1 - mid.astype(F32)).astype(BF16)
    return hi, mid, lo


def _dot_exact_rhs(x, w_bf):
    n = x.shape[0]
    r = jnp.dot(jnp.concatenate(_split3(x), axis=0), w_bf, preferred_element_type=F32)
    return r[:n] + r[n:2 * n] + r[2 * n:]


def _dot_exact_lhs(w_bf, x):
    n = x.shape[1]
    r = jnp.dot(w_bf, jnp.concatenate(_split3(x), axis=1), preferred_element_type=F32)
    return r[:, :n] + r[:, n:2 * n] + r[:, 2 * n:]


def _dot_x3(a, b):
    n = a.shape[0]
    a_hi = a.astype(BF16)
    a_lo = (a - a_hi.astype(F32)).astype(BF16)
    b_hi = b.astype(BF16)
    b_lo = (b - b_hi.astype(F32)).astype(BF16)
    r = jnp.dot(jnp.concatenate([a_hi, a_lo], axis=0), b_hi, preferred_element_type=F32)
    return r[:n] + r[n:] + jnp.dot(a_hi, b_lo, preferred_element_type=F32)


def _head_ones():
    r = lax.broadcasted_iota(jnp.int32, (_LANES, _LANES), 0) // RW_HEAD
    c = lax.broadcasted_iota(jnp.int32, (_LANES, _LANES), 1) // RW_HEAD
    return jnp.where(r == c, 1.0, 0.0).astype(BF16)


def _rwkv_prep_kernel(r_ref, k_ref, v_ref, lw_ref, a_ref, kk_ref, ka_ref, rk_ref, lb_ref,
                      rp_ref, y0_ref, e_ref, g_ref, h_ref, *, npairs):
    L = RW_CHUNK
    n2 = 2 * L
    pairs = range(npairs)
    sls = [slice(p * _LANES, (p + 1) * _LANES) for p in pairs]
    ones_bd = _head_ones()
    ri = lax.broadcasted_iota(jnp.int32, (n2, n2), 0)
    ci = lax.broadcasted_iota(jnp.int32, (n2, n2), 1)
    strict = ci < ri
    incl = ci <= ri
    diag = ri == ci
    tri = jnp.where(lax.broadcasted_iota(jnp.int32, (L, L), 1)
                    <= lax.broadcasted_iota(jnp.int32, (L, L), 0), 1.0, 0.0).astype(BF16)
    head0 = lax.broadcasted_iota(jnp.int32, (L, _LANES), 1) < RW_HEAD

    def stack(x):
        return jnp.concatenate([jnp.where(head0, x, 0.0), jnp.where(head0, 0.0, x)], axis=0)

    def unstack(x):
        return x[:L] + x[L:]

    r = [r_ref[:, sl] for sl in sls]
    k = [k_ref[:, sl] for sl in sls]
    v = [v_ref[:, sl] for sl in sls]
    lw = [lw_ref[:, sl] for sl in sls]
    a = [a_ref[:, sl] for sl in sls]
    kk = [k[p] * kk_ref[:, sls[p]] for p in pairs]
    kmod = [k[p] * (1.0 + (a[p] - 1.0) * ka_ref[:, sls[p]]) for p in pairs]
    cl = [_dot_exact_lhs(tri, lw[p]) for p in pairs]
    hs = [_dot_exact_rhs(jnp.concatenate([kk[p] * kk[p], r[p] * kmod[p] * rk_ref[:, sls[p]]], axis=0),
                         ones_bd) for p in pairs]
    for p in pairs:
        e_ref[:, sls[p]] = lb_ref[:, sls[p]] + hs[p][L:] * v[p]
    kk = [kk[p] / jnp.maximum(jnp.sqrt(hs[p][:L]), 1e-12) for p in pairs]
    bv = [kk[p] * a[p] for p in pairs]
    cl_end = [c[L - 1:L, :] for c in cl]
    c_inv = [jnp.exp(-c) for c in cl]
    c_rem = [jnp.exp(cl_end[p] - cl[p]) for p in pairs]
    a_s = [stack(-kk[p] * jnp.exp(cl[p] - lw[p])) for p in pairs]
    r_s = [stack(r[p] * jnp.exp(cl[p])) for p in pairs]
    v_s = [stack(v[p]).astype(BF16) for p in pairs]
    bk_t = [jnp.concatenate([stack(bv[p] * c_inv[p]), stack(kmod[p] * c_inv[p])], axis=0)
            .astype(BF16).T for p in pairs]
    ar_bk = [_dot_bf(jnp.concatenate([a_s[p], r_s[p]], axis=0), bk_t[p]) for p in pairs]
    nab = [jnp.where(strict, m[:n2, :n2], 0.0) for m in ar_bk]
    nak = [jnp.where(strict, m[:n2, n2:], 0.0) for m in ar_bk]
    mrb = [jnp.where(incl, m[n2:, :n2], 0.0) for m in ar_bk]
    mrk = [jnp.where(incl, m[n2:, n2:], 0.0) for m in ar_bk]
    xv = [_dot_bf(jnp.concatenate([nak[p], mrk[p], stack(kmod[p] * c_rem[p]).T], axis=0), v_s[p])
          for p in pairs]
    levels = int(math.log2(L)) - 1
    t_inv = [jnp.where(diag, 1.0, m) for m in nab]
    x = [_dot_bf(m, m) for m in nab]
    for lvl in range(levels):
        if lvl < levels - 1:
            tx = [_dot_bf(jnp.concatenate([t_inv[p], x[p]], axis=0), x[p]) for p in pairs]
            t_inv = [t_inv[p] + tx[p][:n2] for p in pairs]
            x = [tx[p][n2:] for p in pairs]
        else:
            t_inv = [t_inv[p] + _dot_bf(t_inv[p], x[p]) for p in pairs]
    pq = [_dot_bf(t_inv[p], jnp.concatenate([a_s[p], xv[p][:n2]], axis=1)) for p in pairs]
    rg = [_dot_bf(jnp.concatenate([mrb[p], stack(bv[p] * c_rem[p]).T], axis=0), pq[p]) for p in pairs]
    for p in pairs:
        rp_ref[:, sls[p]] = unstack(r_s[p] + rg[p][:n2, :_LANES])
        y0_ref[:, sls[p]] = unstack(rg[p][:n2, _LANES:] + xv[p][n2:2 * n2])
        g_ref[p] = unstack(jnp.where(diag, jnp.exp(cl_end[p]), 0.0) + rg[p][n2:, :_LANES])
        h_ref[p] = unstack(rg[p][n2:, _LANES:] + xv[p][2 * n2:])


def _rwkv_scan_kernel(rp_ref, y0_ref, e_ref, gate_ref, g_ref, h_ref, lnw_ref, o_ref, st_ref, *, npairs):
    c = pl.program_id(2)

    @pl.when(c == 0)
    def _():
        st_ref[...] = jnp.zeros(st_ref.shape, F32)

    ones_bd = _head_ones()
    inv_n = 1.0 / RW_HEAD
    pairs = range(npairs)
    sls = [slice(p * _LANES, (p + 1) * _LANES) for p in pairs]
    head0 = lax.broadcasted_iota(jnp.int32, (RW_HEAD, _LANES), 1) < RW_HEAD

    def block_diag(x):
        return jnp.concatenate([jnp.where(head0, x, 0.0), jnp.where(head0, 0.0, x)], axis=0)

    prod = [_dot_x3(jnp.concatenate([block_diag(g_ref[p]), rp_ref[:, sls[p]]], axis=0), st_ref[p])
            for p in pairs]
    for p in pairs:
        st_ref[p] = prod[p][:_LANES] + block_diag(h_ref[p])
    y = [prod[p][_LANES:] + y0_ref[:, sls[p]] for p in pairs]
    mu = [_dot_exact_rhs(y[p], ones_bd) * inv_n for p in pairs]
    yc = [y[p] - mu[p] for p in pairs]
    var = [_dot_exact_rhs(yc[p] * yc[p], ones_bd) * inv_n for p in pairs]
    for p in pairs:
        yn = yc[p] * lax.rsqrt(var[p] + GN_EPS)
        o_ref[:, sls[p]] = ((yn * lnw_ref[:, sls[p]] + e_ref[:, sls[p]])
                            * gate_ref[:, sls[p]]).astype(o_ref.dtype)


def _rwkv_core(r, k, v, lw, a, gate, k_k, k_a, r_k, lnx_w, lnx_b, *, B):
    M, D = r.shape
    S = M // B
    L = RW_CHUNK
    nc = S // L
    ngroups = D // _LANES
    np1 = _pick(ngroups, (8, 4, 2, 1))
    np2 = _pick(ngroups, (16, 8, 4, 2, 1))
    vec = lambda t: t.astype(F32).reshape(1, D)

    w1 = np1 * _LANES
    tile1 = pl.BlockSpec((L, w1), lambda b, c, g: (b * nc + c, g))
    par1 = pl.BlockSpec((1, w1), lambda b, c, g: (0, g))
    mat1 = pl.BlockSpec((None, None, np1, RW_HEAD, _LANES), lambda b, c, g: (b, c, g, 0, 0))
    md = jax.ShapeDtypeStruct((M, D), F32)
    gh = jax.ShapeDtypeStruct((B, nc, ngroups, RW_HEAD, _LANES), F32)
    rp, y0, e, gm, hm = pl.pallas_call(
        functools.partial(_rwkv_prep_kernel, npairs=np1),
        out_shape=(md, md, md, gh, gh),
        grid=(B, nc, ngroups // np1),
        in_specs=[tile1] * 5 + [par1] * 4,
        out_specs=(tile1, tile1, tile1, mat1, mat1),
        compiler_params=_params(("parallel", "parallel", "parallel")),
        name="rwkv_chunk_prep",
    )(r, k, v, lw, a, vec(k_k), vec(k_a), vec(r_k), vec(lnx_b))

    w2 = np2 * _LANES
    tile2 = pl.BlockSpec((L, w2), lambda b, g, c: (b * nc + c, g))
    par2 = pl.BlockSpec((1, w2), lambda b, g, c: (0, g))
    mat2 = pl.BlockSpec((None, None, np2, RW_HEAD, _LANES), lambda b, g, c: (b, c, g, 0, 0))
    return pl.pallas_call(
        functools.partial(_rwkv_scan_kernel, npairs=np2),
        out_shape=jax.ShapeDtypeStruct((M, D), BF16),
        grid=(B, ngroups // np2, nc),
        in_specs=[tile2] * 4 + [mat2, mat2, par2],
        out_specs=tile2,
        scratch_shapes=[pltpu.VMEM((np2, _LANES, _LANES), F32)],
        compiler_params=_params(("parallel", "parallel", "arbitrary")),
        name="rwkv_state_scan",
    )(rp, y0, e, gate, gm, hm, vec(lnx_w))


def kernel(x, mem, positions, rel_bias, da_w_qkv, da_lam, da_subln, da_w_o, rw_mix, rw_w_rkv, rw_w0, rw_w1, rw_w2, rw_a0, rw_a1, rw_a2, rw_g1, rw_g2, rw_k_k, rw_k_a, rw_r_k, rw_lnx_w, rw_lnx_b, rw_w_o, ca_w_q, ca_w_kv, ca_w_o, ffn_w_up, ffn_conv_w, ffn_conv_b, ffn_w_down, ln_g, ln_b):
    B, S, D = x.shape
    M = B * S
    depth = ln_g.shape[0]
    alpha = (2 * depth) ** 0.25
    bf = lambda t: t.astype(BF16)
    memb = bf(mem).reshape(B * mem.shape[1], D)
    F = ffn_w_down.shape[1]
    tk_down = F // 2 if (F // 2) % _LANES == 0 else F

    xf = x.reshape(M, D)
    pending = None
    for i in range(depth):
        j = i // 2
        if i % 2 == 0:
            if pending is None:
                xb = bf(xf)
            else:
                xf, xb = _deepnorm(xf, *pending, alpha, with_bf16=True)
            lam_init = 0.8 - 0.6 * math.exp(-0.3 * i)
            qkv = _matmul(xb, bf(da_w_qkv[j]), BF16, scaled_cols=D,
                          col_scale=DA_HEAD_DIM ** -0.5 * _LOG2E)
            att = _diff_attention(qkv.reshape(B, S, 3 * D), positions, rel_bias, da_lam[j],
                                  da_subln[j], lam_init)
            h = _matmul(att.reshape(M, D), bf(da_w_o[j]), F32)
        else:
            xf, (xr, xw, xk, xv, xa, xg) = _norm_token_shift_mix(
                xf, *pending, rw_mix[j].astype(F32), alpha, S=S)
            r = _matmul(xr, bf(rw_w_rkv[j, 0]), F32)
            k = _matmul(xk, bf(rw_w_rkv[j, 1]), F32)
            v = _matmul(xv, bf(rw_w_rkv[j, 2]), F32)
            lw = _lora(xw, bf(rw_w1[j]), bf(rw_w2[j]), rw_w0[j].astype(F32), "decay")
            a = _lora(xa, bf(rw_a1[j]), bf(rw_a2[j]), rw_a0[j].astype(F32), "lr")
            g = _lora(xg, bf(rw_g1[j]), bf(rw_g2[j]), jnp.zeros((D,), F32), "gate")
            o = _rwkv_core(r, k, v, lw, a, g, rw_k_k[j], rw_k_a[j], rw_r_k[j].reshape(D),
                           rw_lnx_w[j], rw_lnx_b[j], B=B)
            h = _matmul(o, bf(rw_w_o[j]), F32)

        kv = _matmul(memb, bf(ca_w_kv[i]), BF16).reshape(B, mem.shape[1], 2 * CA_DIM)
        xf, xb = _norm_cross_attention(xf, h, ln_g[i, 0], ln_b[i, 0], kv, bf(ca_w_q[i]),
                                       bf(ca_w_o[i]), ln_g[i, 1], ln_b[i, 1], alpha, B=B)

        act = _ffn_up(xb, bf(ffn_w_up[i]), ffn_conv_w[i].astype(F32), ffn_conv_b[i].astype(F32), S=S)
        h = _matmul(act, bf(ffn_w_down[i]), F32, tn=512, tk=tk_down)
        pending = (h, ln_g[i, 2], ln_b[i, 2])
    (xf,) = _deepnorm(xf, *pending, alpha, with_bf16=False)
    return xf.reshape(B, S, D)
```

```python
import functools
import math

import jax
import jax.numpy as jnp
from jax import lax
from jax.experimental import pallas as pl
from jax.experimental.pallas import tpu as pltpu

F32 = jnp.float32
BF16 = jnp.bfloat16

_VMEM_LIMIT_BYTES = 56 * 1024 * 1024
_LANES = 128

DA_HEAD_DIM = 128
N_BUCKETS = 32
MAX_DIST = 128
RW_HEAD = 64
GN_EPS = 64e-5
CA_HEADS = 4
CA_HEAD_DIM = 128
CA_DIM = CA_HEADS * CA_HEAD_DIM
CONV_W = 3
LN_EPS = 1e-5
RW_CHUNK = 64

_ATTN_TRIP_KEYS = 2048
_NEG = float(jnp.finfo(jnp.float32).min)
_LOG2E = math.log2(math.e)


def _params(sem):
    return pltpu.CompilerParams(dimension_semantics=sem, vmem_limit_bytes=_VMEM_LIMIT_BYTES)


def _pick(n, prefs):
    for p in prefs:
        if n % p == 0:
            return p
    return n


def _mm_kernel(x_ref, w_ref, o_ref, *scratch, nk, scaled_tiles, col_scale):
    part = jnp.dot(x_ref[...], w_ref[...], preferred_element_type=F32)

    def finish(acc):
        if scaled_tiles:
            acc = acc * jnp.where(pl.program_id(1) < scaled_tiles, col_scale, 1.0)
        o_ref[...] = acc.astype(o_ref.dtype)

    if nk == 1:
        finish(part)
        return
    (acc_ref,) = scratch
    k = pl.program_id(2)

    @pl.when(k == 0)
    def _():
        acc_ref[...] = part

    @pl.when(jnp.logical_and(k > 0, k < nk - 1))
    def _():
        acc_ref[...] += part

    @pl.when(k == nk - 1)
    def _():
        finish(acc_ref[...] + part)


def _matmul(x, w, out_dtype, *, tm=1024, tn=1024, tk=None, scaled_cols=0, col_scale=1.0):
    M, K = x.shape
    N = w.shape[1]
    tm = _pick(M, (tm, 512, 256, 128))
    tn = _pick(N, (tn, 512, 256, 128))
    tk = K if tk is None else tk
    nk = K // tk
    assert K % tk == 0 and scaled_cols % tn == 0
    scratch = [pltpu.VMEM((tm, tn), F32)] if nk > 1 else []
    return pl.pallas_call(
        functools.partial(_mm_kernel, nk=nk, scaled_tiles=scaled_cols // tn, col_scale=col_scale),
        out_shape=jax.ShapeDtypeStruct((M, N), out_dtype),
        grid=(M // tm, N // tn, nk),
        in_specs=[pl.BlockSpec((tm, tk), lambda i, j, k: (i, k)),
                  pl.BlockSpec((tk, tn), lambda i, j, k: (k, j))],
        out_specs=pl.BlockSpec((tm, tn), lambda i, j, k: (i, j)),
        scratch_shapes=scratch,
        compiler_params=_params(("parallel", "parallel", "arbitrary")),
        name="matmul",
    )(x, w)


def _mm_f32w_kernel(x_ref, w_ref, o_ref, *, scaled_tiles, col_scale):
    acc = jnp.dot(x_ref[...], w_ref[...].astype(BF16), preferred_element_type=F32)
    if scaled_tiles:
        acc = acc * jnp.where(pl.program_id(1) < scaled_tiles, col_scale, 1.0)
    o_ref[...] = acc.astype(o_ref.dtype)


def _matmul_f32w(x, w_all, lead, out_dtype, *, tm=1024, tn=512, scaled_cols=0, col_scale=1.0):
    M, K = x.shape
    N = w_all.shape[-1]
    tm = _pick(M, (tm, 512, 256, 128))
    tn = _pick(N, (tn, 256, 128))
    assert scaled_cols % tn == 0 and len(lead) == w_all.ndim - 2
    return pl.pallas_call(
        functools.partial(_mm_f32w_kernel, scaled_tiles=scaled_cols // tn, col_scale=col_scale),
        out_shape=jax.ShapeDtypeStruct((M, N), out_dtype),
        grid=(M // tm, N // tn),
        in_specs=[pl.BlockSpec((tm, K), lambda i, j: (i, 0)),
                  pl.BlockSpec((None,) * len(lead) + (K, tn), lambda i, j: tuple(lead) + (0, j))],
        out_specs=pl.BlockSpec((tm, tn), lambda i, j: (i, j)),
        compiler_params=_params(("parallel", "arbitrary")),
        name="matmul_f32w",
    )(x, w_all.astype(F32))


def _layer_norm(z, g, b):
    mu = jnp.mean(z, axis=-1, keepdims=True)
    zc = z - mu
    var = jnp.mean(zc * zc, axis=-1, keepdims=True)
    return zc * lax.rsqrt(var + LN_EPS) * g + b


def _ln_kernel(x_ref, h_ref, g_ref, b_ref, o_ref, *bf16_out, alpha):
    y = _layer_norm(alpha * x_ref[...] + h_ref[...], g_ref[...], b_ref[...])
    o_ref[...] = y
    for ob_ref in bf16_out:
        ob_ref[...] = y.astype(BF16)


def _deepnorm(x, h, g, b, alpha, *, with_bf16):
    M, D = x.shape
    tm = _pick(M, (256, 128))
    row = pl.BlockSpec((tm, D), lambda i: (i, 0))
    vec = pl.BlockSpec((1, D), lambda i: (0, 0))
    shapes = (jax.ShapeDtypeStruct((M, D), F32),) + ((jax.ShapeDtypeStruct((M, D), BF16),) if with_bf16 else ())
    return pl.pallas_call(
        functools.partial(_ln_kernel, alpha=alpha),
        out_shape=shapes,
        grid=(M // tm,),
        in_specs=[row, row, vec, vec],
        out_specs=tuple(row for _ in shapes),
        compiler_params=_params(("parallel",)),
        name="deepnorm",
    )(x, h, g.reshape(1, D), b.reshape(1, D))


def _t5_buckets(n_dist):
    max_exact = N_BUCKETS // 2
    out = []
    for n in range(n_dist):
        if n < max_exact:
            out.append(n)
            continue
        val = math.log(n / max_exact) / math.log(MAX_DIST / max_exact) * (N_BUCKETS - max_exact)
        assert n == max_exact or n >= MAX_DIST or abs(val - round(val)) > 1e-3
        out.append(min(max_exact + int(val), N_BUCKETS - 1))
    return out


def _attn_kernel(qmin_ref, qmax_ref, kmin_ref, kmax_ref, last_ref, nfar_ref,
                 q_ref, k_ref, v_ref, qp_ref, kp_ref, tbl_ref, lam_ref, sub_ref,
                 o_ref, m_ref, l_ref, acc_ref, *, tk, tks, group, far_dist, lam_init):
    b = pl.program_id(0)
    i = pl.program_id(2)
    d = DA_HEAD_DIM
    tq = q_ref.shape[0]
    qmn = qmin_ref[b, i]
    qmx = qmax_ref[b, i]
    far_bias = tbl_ref[:, _LANES - 1:_LANES]

    m_ref[...] = jnp.full(m_ref.shape, _NEG, F32)
    l_ref[...] = jnp.zeros(l_ref.shape, F32)
    acc_ref[...] = jnp.zeros(acc_ref.shape, F32)

    def scores(r0, width):
        kblk = k_ref[pl.ds(r0, width), :]
        return [lax.dot_general(q_ref[:, mi * d:(mi + 1) * d], kblk[:, mi * d:(mi + 1) * d],
                                (((1,), (1,)), ((), ())), preferred_element_type=F32)
                for mi in range(2)]

    def update(r0, width, s, near):
        nrep = width // _LANES
        vblk = v_ref[pl.ds(r0, width), :]
        if near:
            n = qp_ref[...] - kp_ref[:, pl.ds(r0, width)]
            idx = jnp.clip(n, 0, _LANES - 1)
            table = jnp.broadcast_to(tbl_ref[...], (tq, _LANES))
            bias = jnp.concatenate(
                [jnp.take_along_axis(table, idx[:, c * _LANES:(c + 1) * _LANES], axis=1)
                 for c in range(nrep)], axis=1)
            keep = n >= 0
            s = [jnp.where(keep, sm + bias, _NEG) for sm in s]
            shift_bias = 0.0
        else:
            shift_bias = far_bias
        maps = range(2)
        m_prev = [m_ref[mi] for mi in maps]
        m_new = [jnp.maximum(m_prev[mi], jnp.max(s[mi], axis=-1, keepdims=True) + shift_bias)
                 for mi in maps]
        alpha = [jnp.exp2(m_prev[mi] - m_new[mi]) for mi in maps]
        p = [jnp.exp2(s[mi] - jnp.tile(m_new[mi] - shift_bias, (1, nrep))) for mi in maps]
        for mi in maps:
            psum = p[mi][:, :_LANES]
            for c in range(1, nrep):
                psum = psum + p[mi][:, c * _LANES:(c + 1) * _LANES]
            l_ref[mi] = alpha[mi] * l_ref[mi] + psum
            m_ref[mi] = m_new[mi]
        pv = [jnp.dot(p[mi].astype(BF16), vblk, preferred_element_type=F32) for mi in maps]
        for mi in maps:
            acc_ref[mi] = jnp.tile(alpha[mi], (1, 2 * d // _LANES)) * acc_ref[mi] + pv[mi]

    def far_group(j0, n):
        start = lambda u: pl.multiple_of((j0 + u) * tk, tk)
        s = scores(start(0), tk)
        for u in range(n):
            s_next = scores(start(u + 1), tk) if u + 1 < n else None
            update(start(u), tk, s, False)
            s = s_next

    def group_body(t, carry):
        far_group(group * t, group)
        return carry

    nfar = nfar_ref[b, i]
    ngroup = nfar // group
    lax.fori_loop(0, ngroup, group_body, 0)
    ngrouped = group * ngroup
    for half in (group // 2, group // 4):
        if half >= 1:
            take = (nfar - ngrouped) // half

            @pl.when(take == 1)
            def _(start=ngrouped, half=half):
                far_group(start, half)

            ngrouped = ngrouped + half * take

    def single_body(j, carry):
        r0 = pl.multiple_of(j * tks, tks)
        active = kmin_ref[b, j] <= qmx
        far = (qmn - kmax_ref[b, j]) >= far_dist

        @pl.when(jnp.logical_and(active, far))
        def _():
            update(r0, tks, scores(r0, tks), False)

        @pl.when(jnp.logical_and(active, jnp.logical_not(far)))
        def _():
            update(r0, tks, scores(r0, tks), True)

        return carry

    first = ngrouped * (tk // tks)
    stop = last_ref[b, i] + 1
    paired = (stop - first) >= 2
    single_stop = jnp.where(paired, stop - 2, stop)
    lax.fori_loop(first, single_stop, single_body, 0)

    @pl.when(paired)
    def _():
        r_a = pl.multiple_of((stop - 2) * tks, tks)
        r_b = pl.multiple_of((stop - 1) * tks, tks)
        s_a = scores(r_a, tks)
        s_b = scores(r_b, tks)
        update(r_a, tks, s_a, True)
        update(r_b, tks, s_b, True)

    lv = lam_ref[...]
    lam = (jnp.exp(jnp.sum(lv[0:1] * lv[1:2], axis=-1, keepdims=True))
           - jnp.exp(jnp.sum(lv[2:3] * lv[3:4], axis=-1, keepdims=True)) + lam_init)
    l1 = jnp.sum(l_ref[0], axis=-1, keepdims=True)
    l2 = jnp.sum(l_ref[1], axis=-1, keepdims=True)
    o = acc_ref[0] / l1 - lam * (acc_ref[1] / l2)
    o = o * lax.rsqrt(jnp.mean(o * o, axis=-1, keepdims=True) + LN_EPS)
    o_ref[...] = (o * sub_ref[...] * (1.0 - lam_init)).astype(o_ref.dtype)


def _diff_attention(qkv, positions, rel_bias, lam_vecs, subln_w, lam_init, *, tq=512, tk=1024, tks=512):
    B, S, D3 = qkv.shape
    D = D3 // 3
    H = D // (2 * DA_HEAD_DIM)
    w = 2 * DA_HEAD_DIM
    tq = _pick(S, (tq, 256, 128))
    tk = _pick(S, (tk, 256, 128))
    tks = min(tks, tk)
    nq, nk, nks = S // tq, S // tk, S // tks
    buckets = _t5_buckets(_LANES)
    assert buckets[-1] == N_BUCKETS - 1
    far_dist = buckets.index(N_BUCKETS - 1)
    table = rel_bias.astype(F32)[jnp.array(buckets, jnp.int32), :].T * _LOG2E
    table = table.reshape(H, 1, _LANES)

    pq = positions.reshape(B, nq, tq)
    pk = positions.reshape(B, nks, tks)
    qmin, qmax = pq.min(-1), pq.max(-1)
    kmin, kmax = pk.min(-1), pk.max(-1)
    needed = kmin[:, None, :] <= qmax[:, :, None]
    last = jnp.max(jnp.where(needed, jnp.arange(nks, dtype=jnp.int32), 0), axis=-1).astype(jnp.int32)
    kmax_wide = kmax.reshape(B, nk, tk // tks).max(-1)
    far = (qmin[:, :, None] - kmax_wide[:, None, :]) >= far_dist
    nfar = jnp.sum(jnp.cumprod(far.astype(jnp.int32), axis=-1), axis=-1).astype(jnp.int32)
    nfar = jnp.minimum(nfar, (last + 1) // (tk // tks))

    grid_spec = pltpu.PrefetchScalarGridSpec(
        num_scalar_prefetch=6,
        grid=(B, H, nq),
        in_specs=[
            pl.BlockSpec((None, tq, w), lambda b, h, i, *_: (b, i, h)),
            pl.BlockSpec((None, S, w), lambda b, h, i, *_: (b, 0, H + h)),
            pl.BlockSpec((None, S, w), lambda b, h, i, *_: (b, 0, 2 * H + h)),
            pl.BlockSpec((None, tq, 1), lambda b, h, i, *_: (b, i, 0)),
            pl.BlockSpec((None, 1, S), lambda b, h, i, *_: (b, 0, 0)),
            pl.BlockSpec((None, 1, _LANES), lambda b, h, i, *_: (h, 0, 0)),
            pl.BlockSpec((4, DA_HEAD_DIM), lambda b, h, i, *_: (0, 0)),
            pl.BlockSpec((1, w), lambda b, h, i, *_: (0, 0)),
        ],
        out_specs=pl.BlockSpec((None, tq, w), lambda b, h, i, *_: (b, i, h)),
        scratch_shapes=[pltpu.VMEM((2, tq, _LANES), F32), pltpu.VMEM((2, tq, _LANES), F32),
                        pltpu.VMEM((2, tq, w), F32)],
    )
    return pl.pallas_call(
        functools.partial(_attn_kernel, tk=tk, tks=tks, group=max(2, _ATTN_TRIP_KEYS // tk),
                          far_dist=far_dist, lam_init=lam_init),
        out_shape=jax.ShapeDtypeStruct((B, S, D), BF16),
        grid_spec=grid_spec,
        compiler_params=_params(("parallel", "parallel", "arbitrary")),
        name="diff_attention",
    )(qmin, qmax, kmin, kmax, last, nfar,
      qkv, qkv, qkv, positions.reshape(B, S, 1), positions.reshape(B, 1, S),
      table, lam_vecs.astype(F32), subln_w.astype(F32).reshape(1, w))


def _xattn_kernel(x_ref, h_ref, g0_ref, b0_ref, wq_ref, kv_ref, wo_ref, g_ref, b_ref,
                  o_ref, ob_ref, *, alpha):
    dh = CA_HEAD_DIM
    x1 = _layer_norm(alpha * x_ref[...] + h_ref[...], g0_ref[...], b0_ref[...])
    q = jnp.dot(x1.astype(BF16), wq_ref[...], preferred_element_type=F32) * (dh ** -0.5)
    q = q.astype(BF16)
    heads = []
    for hh in range(CA_HEADS):
        kh = kv_ref[:, hh * dh:(hh + 1) * dh]
        vh = kv_ref[:, CA_DIM + hh * dh:CA_DIM + (hh + 1) * dh]
        s = lax.dot_general(q[:, hh * dh:(hh + 1) * dh], kh, (((1,), (1,)), ((), ())),
                            preferred_element_type=F32)
        p = jnp.exp(s - jnp.max(s, axis=-1, keepdims=True))
        p = p / jnp.sum(p, axis=-1, keepdims=True)
        heads.append(jnp.dot(p.astype(BF16), vh, preferred_element_type=F32))
    o = jnp.concatenate(heads, axis=-1).astype(BF16)
    hcat = jnp.dot(o, wo_ref[...], preferred_element_type=F32)
    y = _layer_norm(alpha * x1 + hcat, g_ref[...], b_ref[...])
    o_ref[...] = y
    ob_ref[...] = y.astype(BF16)


def _norm_cross_attention(x, h, g0, b0, kv, w_q, w_o, g, b, alpha, *, B):
    M, D = x.shape
    S = M // B
    NM = kv.shape[1]
    tm = _pick(S, (256, 128))
    nt = S // tm
    row = pl.BlockSpec((tm, D), lambda i: (i, 0))
    vec = pl.BlockSpec((1, D), lambda i: (0, 0))
    return pl.pallas_call(
        functools.partial(_xattn_kernel, alpha=alpha),
        out_shape=(jax.ShapeDtypeStruct((M, D), F32), jax.ShapeDtypeStruct((M, D), BF16)),
        grid=(M // tm,),
        in_specs=[row, row, vec, vec,
                  pl.BlockSpec((D, CA_DIM), lambda i: (0, 0)),
                  pl.BlockSpec((None, NM, 2 * CA_DIM), lambda i: (i // nt, 0, 0)),
                  pl.BlockSpec((CA_DIM, D), lambda i: (0, 0)),
                  vec, vec],
        out_specs=(row, row),
        compiler_params=_params(("parallel",)),
        name="norm_cross_attention",
    )(x, h, g0.reshape(1, D), b0.reshape(1, D), w_q, kv, w_o, g.reshape(1, D), b.reshape(1, D))


def _ffn_up_kernel(x_ref, wg_ref, wv_ref, cwg_ref, cwv_ref, cbg_ref, cbv_ref, o_ref,
                   carry_ref, *, ts, tiles_per_seq):
    i = pl.program_id(0)
    j = pl.program_id(1)
    nsub = x_ref.shape[0] // ts
    row = lax.broadcasted_iota(jnp.int32, (ts, 1), 0)
    wg = wg_ref[...].astype(BF16)
    wv = wv_ref[...].astype(BF16)

    def up(u):
        x = x_ref[u * ts:(u + 1) * ts, :]
        return (jnp.dot(x, wg, preferred_element_type=F32), jnp.dot(x, wv, preferred_element_type=F32))

    def conv(hcur, prev, cw_ref, cb_ref):
        p1 = prev[7:8, :]
        p2 = prev[6:7, :]
        h1 = jnp.where(row == 0, p1, pltpu.roll(hcur, 1, axis=0))
        h2 = jnp.where(row == 0, p2, jnp.where(row == 1, p1, pltpu.roll(hcur, 2, axis=0)))
        return cw_ref[0:1, :] * h2 + cw_ref[1:2, :] * h1 + cw_ref[2:3, :] * hcur + cb_ref[...]

    prev = (carry_ref[0, j], carry_ref[1, j])
    raw = up(0)
    for u in range(nsub):
        raw_next = up(u + 1) if u + 1 < nsub else None
        seq_start = ((i * nsub + u) % tiles_per_seq) == 0
        gate = conv(raw[0], jnp.where(seq_start, 0.0, prev[0]), cwg_ref, cbg_ref)
        val = conv(raw[1], jnp.where(seq_start, 0.0, prev[1]), cwv_ref, cbv_ref)
        o_ref[u * ts:(u + 1) * ts, :] = (gate * jax.nn.sigmoid(gate) * val).astype(o_ref.dtype)
        prev = (raw[0][ts - 8:, :], raw[1][ts - 8:, :])
        raw = raw_next
    carry_ref[0, j] = prev[0]
    carry_ref[1, j] = prev[1]


def _ffn_up(xb, w_up_all, layer, conv_w, conv_b, *, S):
    M, D = xb.shape
    F2 = w_up_all.shape[2]
    F = F2 // 2
    tn = 256
    assert F % tn == 0
    nf = F // tn
    ts = _pick(S, (1024, 512, 256, 128))
    tm = 2 * ts if M % (2 * ts) == 0 else ts
    conv_b = conv_b.reshape(1, F2)
    return pl.pallas_call(
        functools.partial(_ffn_up_kernel, ts=ts, tiles_per_seq=S // ts),
        out_shape=jax.ShapeDtypeStruct((M, F), BF16),
        grid=(M // tm, nf),
        in_specs=[pl.BlockSpec((tm, D), lambda i, j: (i, 0), pipeline_mode=pl.Buffered(1)),
                  pl.BlockSpec((None, D, tn), lambda i, j: (layer, 0, j)),
                  pl.BlockSpec((None, D, tn), lambda i, j: (layer, 0, nf + j)),
                  pl.BlockSpec((CONV_W, tn), lambda i, j: (0, j)),
                  pl.BlockSpec((CONV_W, tn), lambda i, j: (0, nf + j)),
                  pl.BlockSpec((1, tn), lambda i, j: (0, j)),
                  pl.BlockSpec((1, tn), lambda i, j: (0, nf + j))],
        out_specs=pl.BlockSpec((tm, tn), lambda i, j: (i, j)),
        scratch_shapes=[pltpu.VMEM((2, nf, 8, tn), F32)],
        compiler_params=_params(("arbitrary", "arbitrary")),
        name="ffn_up_conv_glu",
    )(xb, w_up_all, w_up_all, conv_w, conv_w, conv_b, conv_b)


def _mix_kernel(x_ref, h_ref, g_ref, b_ref, mix_ref, xo_ref, *rest, alpha, tiles_per_seq):
    outs, carry_ref = rest[:-1], rest[-1]
    i = pl.program_id(0)
    tm = x_ref.shape[0]
    x = _layer_norm(alpha * x_ref[...] + h_ref[...], g_ref[...], b_ref[...])
    xo_ref[...] = x
    row = lax.broadcasted_iota(jnp.int32, (tm, 1), 0)
    prev_last = jnp.where((i % tiles_per_seq) == 0, 0.0, carry_ref[7:8, :])
    xx = jnp.where(row == 0, prev_last, pltpu.roll(x, 1, axis=0)) - x
    carry_ref[...] = x[tm - 8:, :]
    for m, o_ref in enumerate(outs):
        o_ref[...] = (x + xx * mix_ref[m:m + 1, :]).astype(BF16)


def _norm_token_shift_mix(x, h, g, b, mix, alpha, *, S):
    M, D = x.shape
    tm = _pick(S, (128,))
    n = mix.shape[0]
    row = pl.BlockSpec((tm, D), lambda i: (i, 0))
    vec = pl.BlockSpec((1, D), lambda i: (0, 0))
    outs = pl.pallas_call(
        functools.partial(_mix_kernel, alpha=alpha, tiles_per_seq=S // tm),
        out_shape=(jax.ShapeDtypeStruct((M, D), F32),)
        + tuple(jax.ShapeDtypeStruct((M, D), BF16) for _ in range(n)),
        grid=(M // tm,),
        in_specs=[row, row, vec, vec, pl.BlockSpec((n, D), lambda i: (0, 0))],
        out_specs=(row,) + tuple(row for _ in range(n)),
        scratch_shapes=[pltpu.VMEM((8, D), F32)],
        compiler_params=_params(("arbitrary",)),
        name="norm_token_shift_mix",
    )(x, h, g.reshape(1, D), b.reshape(1, D), mix)
    return outs[0], outs[1:]


def _lora_kernel(x_ref, w1_ref, w2_ref, b_ref, o_ref, *, mode):
    t = jnp.dot(x_ref[...], w1_ref[...], preferred_element_type=F32)
    if mode == "decay":
        t = jnp.tanh(t)
    elif mode == "gate":
        t = jax.nn.sigmoid(t)
    z = jnp.dot(t.astype(BF16), w2_ref[...], preferred_element_type=F32) + b_ref[...]
    if mode == "decay":
        u = -z
        sp = jnp.maximum(u, 0.0) + jnp.log(1.0 + jnp.exp(-jnp.abs(u)))
        z = -jnp.exp(-sp - 0.5)
    elif mode == "lr":
        z = jax.nn.sigmoid(z)
    o_ref[...] = z


def _lora(xb, w1, w2, bias, mode):
    M, D = xb.shape
    R = w1.shape[1]
    tm = _pick(M, (512, 256, 128))
    row = pl.BlockSpec((tm, D), lambda i: (i, 0))
    return pl.pallas_call(
        functools.partial(_lora_kernel, mode=mode),
        out_shape=jax.ShapeDtypeStruct((M, D), F32),
        grid=(M // tm,),
        in_specs=[row, pl.BlockSpec((D, R), lambda i: (0, 0)),
                  pl.BlockSpec((R, D), lambda i: (0, 0)),
                  pl.BlockSpec((1, D), lambda i: (0, 0))],
        out_specs=row,
        compiler_params=_params(("parallel",)),
        name="lora_" + mode,
    )(xb, w1, w2, bias.reshape(1, D))


def _dot_bf(a, b):
    return jnp.dot(a.astype(BF16), b.astype(BF16), preferred_element_type=F32)


def _split3(x):
    hi = x.astype(BF16)
    r1 = x - hi.astype(F32)
    mid = r1.astype(BF16)
    lo = (r1 - mid.astype(F32)).astype(BF16)
    return hi, mid, lo


def _dot_exact_rhs(x, w_bf):
    n = x.shape[0]
    r = jnp.dot(jnp.concatenate(_split3(x), axis=0), w_bf, preferred_element_type=F32)
    return r[:n] + r[n:2 * n] + r[2 * n:]


def _dot_exact_lhs(w_bf, x):
    n = x.shape[1]
    r = jnp.dot(w_bf, jnp.concatenate(_split3(x), axis=1), preferred_element_type=F32)
    return r[:, :n] + r[:, n:2 * n] + r[:, 2 * n:]


def _dot_x3(a, b):
    n = a.shape[0]
    a_hi = a.astype(BF16)
    a_lo = (a - a_hi.astype(F32)).astype(BF16)
    b_hi = b.astype(BF16)
    b_lo = (b - b_hi.astype(F32)).astype(BF16)
    r = jnp.dot(jnp.concatenate([a_hi, a_lo], axis=0), b_hi, preferred_element_type=F32)
    return r[:n] + r[n:] + jnp.dot(a_hi, b_lo, preferred_element_type=F32)


def _head_ones():
    r = lax.broadcasted_iota(jnp.int32, (_LANES, _LANES), 0) // RW_HEAD
    c = lax.broadcasted_iota(jnp.int32, (_LANES, _LANES), 1) // RW_HEAD
    return jnp.where(r == c, 1.0, 0.0).astype(BF16)


def _rwkv_prep_kernel(r_ref, k_ref, v_ref, lw_ref, a_ref, kk_ref, ka_ref, rk_ref, lb_ref,
                      rp_ref, y0_ref, e_ref, g_ref, h_ref, *, npairs):
    L = RW_CHUNK
    n2 = 2 * L
    pairs = range(npairs)
    sls = [slice(p * _LANES, (p + 1) * _LANES) for p in pairs]
    ones_bd = _head_ones()
    ri = lax.broadcasted_iota(jnp.int32, (n2, n2), 0)
    ci = lax.broadcasted_iota(jnp.int32, (n2, n2), 1)
    strict = ci < ri
    incl = ci <= ri
    diag = ri == ci
    tri = jnp.where(lax.broadcasted_iota(jnp.int32, (L, L), 1)
                    <= lax.broadcasted_iota(jnp.int32, (L, L), 0), 1.0, 0.0).astype(BF16)
    head0 = lax.broadcasted_iota(jnp.int32, (L, _LANES), 1) < RW_HEAD

    def stack(x):
        return jnp.concatenate([jnp.where(head0, x, 0.0), jnp.where(head0, 0.0, x)], axis=0)

    def unstack(x):
        return x[:L] + x[L:]

    r = [r_ref[:, sl] for sl in sls]
    k = [k_ref[:, sl] for sl in sls]
    v = [v_ref[:, sl] for sl in sls]
    lw = [lw_ref[:, sl] for sl in sls]
    a = [a_ref[:, sl] for sl in sls]
    kk = [k[p] * kk_ref[:, sls[p]] for p in pairs]
    kmod = [k[p] * (1.0 + (a[p] - 1.0) * ka_ref[:, sls[p]]) for p in pairs]
    cl = [_dot_exact_lhs(tri, lw[p]) for p in pairs]
    hs = [_dot_exact_rhs(jnp.concatenate([kk[p] * kk[p], r[p] * kmod[p] * rk_ref[:, sls[p]]], axis=0),
                         ones_bd) for p in pairs]
    for p in pairs:
        e_ref[:, sls[p]] = lb_ref[:, sls[p]] + hs[p][L:] * v[p]
    kk = [kk[p] / jnp.maximum(jnp.sqrt(hs[p][:L]), 1e-12) for p in pairs]
    bv = [kk[p] * a[p] for p in pairs]
    cl_end = [c[L - 1:L, :] for c in cl]
    c_inv = [jnp.exp(-c) for c in cl]
    c_rem = [jnp.exp(cl_end[p] - cl[p]) for p in pairs]
    a_s = [stack(-kk[p] * jnp.exp(cl[p] - lw[p])) for p in pairs]
    r_s = [stack(r[p] * jnp.exp(cl[p])) for p in pairs]
    v_s = [stack(v[p]).astype(BF16) for p in pairs]
    bk_t = [jnp.concatenate([stack(bv[p] * c_inv[p]), stack(kmod[p] * c_inv[p])], axis=0)
            .astype(BF16).T for p in pairs]
    ar_bk = [_dot_bf(jnp.concatenate([a_s[p], r_s[p]], axis=0), bk_t[p]) for p in pairs]
    nab = [jnp.where(strict, m[:n2, :n2], 0.0) for m in ar_bk]
    nak = [jnp.where(strict, m[:n2, n2:], 0.0) for m in ar_bk]
    mrb = [jnp.where(incl, m[n2:, :n2], 0.0) for m in ar_bk]
    mrk = [jnp.where(incl, m[n2:, n2:], 0.0) for m in ar_bk]
    xv = [_dot_bf(jnp.concatenate([nak[p], mrk[p], stack(kmod[p] * c_rem[p]).T], axis=0), v_s[p])
          for p in pairs]
    levels = int(math.log2(L)) - 1
    t_inv = [jnp.where(diag, 1.0, m) for m in nab]
    x = [_dot_bf(m, m) for m in nab]
    for lvl in range(levels):
        if lvl < levels - 1:
            tx = [_dot_bf(jnp.concatenate([t_inv[p], x[p]], axis=0), x[p]) for p in pairs]
            t_inv = [t_inv[p] + tx[p][:n2] for p in pairs]
            x = [tx[p][n2:] for p in pairs]
        else:
            t_inv = [t_inv[p] + _dot_bf(t_inv[p], x[p]) for p in pairs]
    pq = [_dot_bf(t_inv[p], jnp.concatenate([a_s[p], xv[p][:n2]], axis=1)) for p in pairs]
    rg = [_dot_bf(jnp.concatenate([mrb[p], stack(bv[p] * c_rem[p]).T], axis=0), pq[p]) for p in pairs]
    for p in pairs:
        rp_ref[:, sls[p]] = unstack(r_s[p] + rg[p][:n2, :_LANES])
        y0_ref[:, sls[p]] = unstack(rg[p][:n2, _LANES:] + xv[p][n2:2 * n2])
        g_ref[p] = unstack(jnp.where(diag, jnp.exp(cl_end[p]), 0.0) + rg[p][n2:, :_LANES])
        h_ref[p] = unstack(rg[p][n2:, _LANES:] + xv[p][2 * n2:])


def _rwkv_scan_kernel(rp_ref, y0_ref, e_ref, gate_ref, g_ref, h_ref, lnw_ref, o_ref, st_ref, *, npairs):
    c = pl.program_id(2)

    @pl.when(c == 0)
    def _():
        st_ref[...] = jnp.zeros(st_ref.shape, F32)

    ones_bd = _head_ones()
    inv_n = 1.0 / RW_HEAD
    pairs = range(npairs)
    sls = [slice(p * _LANES, (p + 1) * _LANES) for p in pairs]
    head0 = lax.broadcasted_iota(jnp.int32, (RW_HEAD, _LANES), 1) < RW_HEAD

    def block_diag(x):
        return jnp.concatenate([jnp.where(head0, x, 0.0), jnp.where(head0, 0.0, x)], axis=0)

    prod = [_dot_x3(jnp.concatenate([block_diag(g_ref[p]), rp_ref[:, sls[p]]], axis=0), st_ref[p])
            for p in pairs]
    for p in pairs:
        st_ref[p] = prod[p][:_LANES] + block_diag(h_ref[p])
    y = [prod[p][_LANES:] + y0_ref[:, sls[p]] for p in pairs]
    mu = [_dot_exact_rhs(y[p], ones_bd) * inv_n for p in pairs]
    yc = [y[p] - mu[p] for p in pairs]
    var = [_dot_exact_rhs(yc[p] * yc[p], ones_bd) * inv_n for p in pairs]
    for p in pairs:
        yn = yc[p] * lax.rsqrt(var[p] + GN_EPS)
        o_ref[:, sls[p]] = ((yn * lnw_ref[:, sls[p]] + e_ref[:, sls[p]])
                            * gate_ref[:, sls[p]]).astype(o_ref.dtype)


def _rwkv_core(r, k, v, lw, a, gate, k_k, k_a, r_k, lnx_w, lnx_b, *, B):
    M, D = r.shape
    S = M // B
    L = RW_CHUNK
    nc = S // L
    ngroups = D // _LANES
    np1 = _pick(ngroups, (8, 4, 2, 1))
    np2 = _pick(ngroups, (16, 8, 4, 2, 1))
    vec = lambda t: t.astype(F32).reshape(1, D)

    w1 = np1 * _LANES
    tile1 = pl.BlockSpec((L, w1), lambda b, c, g: (b * nc + c, g))
    par1 = pl.BlockSpec((1, w1), lambda b, c, g: (0, g))
    mat1 = pl.BlockSpec((None, None, np1, RW_HEAD, _LANES), lambda b, c, g: (b, c, g, 0, 0))
    md = jax.ShapeDtypeStruct((M, D), F32)
    gh = jax.ShapeDtypeStruct((B, nc, ngroups, RW_HEAD, _LANES), F32)
    rp, y0, e, gm, hm = pl.pallas_call(
        functools.partial(_rwkv_prep_kernel, npairs=np1),
        out_shape=(md, md, md, gh, gh),
        grid=(B, nc, ngroups // np1),
        in_specs=[tile1] * 5 + [par1] * 4,
        out_specs=(tile1, tile1, tile1, mat1, mat1),
        compiler_params=_params(("parallel", "parallel", "parallel")),
        name="rwkv_chunk_prep",
    )(r, k, v, lw, a, vec(k_k), vec(k_a), vec(r_k), vec(lnx_b))

    w2 = np2 * _LANES
    tile2 = pl.BlockSpec((L, w2), lambda b, g, c: (b * nc + c, g))
    par2 = pl.BlockSpec((1, w2), lambda b, g, c: (0, g))
    mat2 = pl.BlockSpec((None, None, np2, RW_HEAD, _LANES), lambda b, g, c: (b, c, g, 0, 0))
    return pl.pallas_call(
        functools.partial(_rwkv_scan_kernel, npairs=np2),
        out_shape=jax.ShapeDtypeStruct((M, D), BF16),
        grid=(B, ngroups // np2, nc),
        in_specs=[tile2] * 4 + [mat2, mat2, par2],
        out_specs=tile2,
        scratch_shapes=[pltpu.VMEM((np2, _LANES, _LANES), F32)],
        compiler_params=_params(("parallel", "parallel", "arbitrary")),
        name="rwkv_state_scan",
    )(rp, y0, e, gate, gm, hm, vec(lnx_w))


def kernel(x, mem, positions, rel_bias, da_w_qkv, da_lam, da_subln, da_w_o, rw_mix, rw_w_rkv, rw_w0, rw_w1, rw_w2, rw_a0, rw_a1, rw_a2, rw_g1, rw_g2, rw_k_k, rw_k_a, rw_r_k, rw_lnx_w, rw_lnx_b, rw_w_o, ca_w_q, ca_w_kv, ca_w_o, ffn_w_up, ffn_conv_w, ffn_conv_b, ffn_w_down, ln_g, ln_b):
    B, S, D = x.shape
    M = B * S
    depth = ln_g.shape[0]
    alpha = (2 * depth) ** 0.25
    bf = lambda t: t.astype(BF16)
    memb = bf(mem).reshape(B * mem.shape[1], D)
    F = ffn_w_down.shape[1]
    tk_down = F // 2 if (F // 2) % _LANES == 0 else F

    xf = x.reshape(M, D)
    pending = None
    for i in range(depth):
        j = i // 2
        if i % 2 == 0:
            if pending is None:
                xb = bf(xf)
            else:
                xf, xb = _deepnorm(xf, *pending, alpha, with_bf16=True)
            lam_init = 0.8 - 0.6 * math.exp(-0.3 * i)
            qkv = _matmul_f32w(xb, da_w_qkv, (j,), BF16, scaled_cols=D,
                               col_scale=DA_HEAD_DIM ** -0.5 * _LOG2E)
            att = _diff_attention(qkv.reshape(B, S, 3 * D), positions, rel_bias, da_lam[j],
                                  da_subln[j], lam_init)
            h = _matmul_f32w(att.reshape(M, D), da_w_o, (j,), F32)
        else:
            xf, (xr, xw, xk, xv, xa, xg) = _norm_token_shift_mix(
                xf, *pending, rw_mix[j].astype(F32), alpha, S=S)
            r = _matmul_f32w(xr, rw_w_rkv, (j, 0), F32)
            k = _matmul_f32w(xk, rw_w_rkv, (j, 1), F32)
            v = _matmul_f32w(xv, rw_w_rkv, (j, 2), F32)
            lw = _lora(xw, bf(rw_w1[j]), bf(rw_w2[j]), rw_w0[j].astype(F32), "decay")
            a = _lora(xa, bf(rw_a1[j]), bf(rw_a2[j]), rw_a0[j].astype(F32), "lr")
            g = _lora(xg, bf(rw_g1[j]), bf(rw_g2[j]), jnp.zeros((D,), F32), "gate")
            o = _rwkv_core(r, k, v, lw, a, g, rw_k_k[j], rw_k_a[j], rw_r_k[j].reshape(D),
                           rw_lnx_w[j], rw_lnx_b[j], B=B)
            h = _matmul_f32w(o, rw_w_o, (j,), F32)

        kv = _matmul(memb, bf(ca_w_kv[i]), BF16).reshape(B, mem.shape[1], 2 * CA_DIM)
        xf, xb = _norm_cross_attention(xf, h, ln_g[i, 0], ln_b[i, 0], kv, bf(ca_w_q[i]),
                                       bf(ca_w_o[i]), ln_g[i, 1], ln_b[i, 1], alpha, B=B)

        act = _ffn_up(xb, ffn_w_up.astype(F32), i, ffn_conv_w[i].astype(F32), ffn_conv_b[i].astype(F32), S=S)
        h = _matmul(act, bf(ffn_w_down[i]), F32, tn=512, tk=tk_down)
        pending = (h, ln_g[i, 2], ln_b[i, 2])
    (xf,) = _deepnorm(xf, *pending, alpha, with_bf16=False)
    return xf.reshape(B, S, D)
```

```python
import functools
import math

import jax
import jax.numpy as jnp
from jax import lax
from jax.experimental import pallas as pl
from jax.experimental.pallas import tpu as pltpu

F32 = jnp.float32
BF16 = jnp.bfloat16

_VMEM_LIMIT_BYTES = 56 * 1024 * 1024
_LANES = 128

DA_HEAD_DIM = 128
N_BUCKETS = 32
MAX_DIST = 128
RW_HEAD = 64
GN_EPS = 64e-5
CA_HEADS = 4
CA_HEAD_DIM = 128
CA_DIM = CA_HEADS * CA_HEAD_DIM
CONV_W = 3
LN_EPS = 1e-5
RW_CHUNK = 64

_ATTN_TRIP_KEYS = 2048
_NEG = float(jnp.finfo(jnp.float32).min)
_LOG2E = math.log2(math.e)


def _params(sem):
    return pltpu.CompilerParams(dimension_semantics=sem, vmem_limit_bytes=_VMEM_LIMIT_BYTES)


def _pick(n, prefs):
    for p in prefs:
        if n % p == 0:
            return p
    return n


def _mm_kernel(x_ref, w_ref, o_ref, *scratch, nk, scaled_tiles, col_scale):
    part = jnp.dot(x_ref[...], w_ref[...], preferred_element_type=F32)

    def finish(acc):
        if scaled_tiles:
            acc = acc * jnp.where(pl.program_id(1) < scaled_tiles, col_scale, 1.0)
        o_ref[...] = acc.astype(o_ref.dtype)

    if nk == 1:
        finish(part)
        return
    (acc_ref,) = scratch
    k = pl.program_id(2)

    @pl.when(k == 0)
    def _():
        acc_ref[...] = part

    @pl.when(jnp.logical_and(k > 0, k < nk - 1))
    def _():
        acc_ref[...] += part

    @pl.when(k == nk - 1)
    def _():
        finish(acc_ref[...] + part)


def _matmul(x, w, out_dtype, *, tm=1024, tn=1024, tk=None, scaled_cols=0, col_scale=1.0):
    M, K = x.shape
    N = w.shape[1]
    tm = _pick(M, (tm, 512, 256, 128))
    tn = _pick(N, (tn, 512, 256, 128))
    tk = K if tk is None else tk
    nk = K // tk
    assert K % tk == 0 and scaled_cols % tn == 0
    scratch = [pltpu.VMEM((tm, tn), F32)] if nk > 1 else []
    return pl.pallas_call(
        functools.partial(_mm_kernel, nk=nk, scaled_tiles=scaled_cols // tn, col_scale=col_scale),
        out_shape=jax.ShapeDtypeStruct((M, N), out_dtype),
        grid=(M // tm, N // tn, nk),
        in_specs=[pl.BlockSpec((tm, tk), lambda i, j, k: (i, k)),
                  pl.BlockSpec((tk, tn), lambda i, j, k: (k, j))],
        out_specs=pl.BlockSpec((tm, tn), lambda i, j, k: (i, j)),
        scratch_shapes=scratch,
        compiler_params=_params(("parallel", "parallel", "arbitrary")),
        name="matmul",
    )(x, w)


def _mm_f32w_kernel(x_ref, w_ref, o_ref, *, scaled_tiles, col_scale):
    acc = jnp.dot(x_ref[...], w_ref[...].astype(BF16), preferred_element_type=F32)
    if scaled_tiles:
        acc = acc * jnp.where(pl.program_id(1) < scaled_tiles, col_scale, 1.0)
    o_ref[...] = acc.astype(o_ref.dtype)


def _matmul_f32w(x, w_all, lead, out_dtype, *, tm=2048, tn=256, scaled_cols=0, col_scale=1.0):
    M, K = x.shape
    N = w_all.shape[-1]
    tm = _pick(M, (tm, 512, 256, 128))
    tn = _pick(N, (tn, 256, 128))
    assert scaled_cols % tn == 0 and len(lead) == w_all.ndim - 2
    return pl.pallas_call(
        functools.partial(_mm_f32w_kernel, scaled_tiles=scaled_cols // tn, col_scale=col_scale),
        out_shape=jax.ShapeDtypeStruct((M, N), out_dtype),
        grid=(M // tm, N // tn),
        in_specs=[pl.BlockSpec((tm, K), lambda i, j: (i, 0)),
                  pl.BlockSpec((None,) * len(lead) + (K, tn), lambda i, j: tuple(lead) + (0, j))],
        out_specs=pl.BlockSpec((tm, tn), lambda i, j: (i, j)),
        compiler_params=_params(("parallel", "arbitrary")),
        name="matmul_f32w",
    )(x, w_all.astype(F32))


def _layer_norm(z, g, b):
    mu = jnp.mean(z, axis=-1, keepdims=True)
    zc = z - mu
    var = jnp.mean(zc * zc, axis=-1, keepdims=True)
    return zc * lax.rsqrt(var + LN_EPS) * g + b


def _ln_kernel(x_ref, h_ref, g_ref, b_ref, o_ref, *bf16_out, alpha):
    y = _layer_norm(alpha * x_ref[...] + h_ref[...], g_ref[...], b_ref[...])
    o_ref[...] = y
    for ob_ref in bf16_out:
        ob_ref[...] = y.astype(BF16)


def _deepnorm(x, h, g, b, alpha, *, with_bf16):
    M, D = x.shape
    tm = _pick(M, (256, 128))
    row = pl.BlockSpec((tm, D), lambda i: (i, 0))
    vec = pl.BlockSpec((1, D), lambda i: (0, 0))
    shapes = (jax.ShapeDtypeStruct((M, D), F32),) + ((jax.ShapeDtypeStruct((M, D), BF16),) if with_bf16 else ())
    return pl.pallas_call(
        functools.partial(_ln_kernel, alpha=alpha),
        out_shape=shapes,
        grid=(M // tm,),
        in_specs=[row, row, vec, vec],
        out_specs=tuple(row for _ in shapes),
        compiler_params=_params(("parallel",)),
        name="deepnorm",
    )(x, h, g.reshape(1, D), b.reshape(1, D))


def _t5_buckets(n_dist):
    max_exact = N_BUCKETS // 2
    out = []
    for n in range(n_dist):
        if n < max_exact:
            out.append(n)
            continue
        val = math.log(n / max_exact) / math.log(MAX_DIST / max_exact) * (N_BUCKETS - max_exact)
        assert n == max_exact or n >= MAX_DIST or abs(val - round(val)) > 1e-3
        out.append(min(max_exact + int(val), N_BUCKETS - 1))
    return out


def _attn_kernel(qmin_ref, qmax_ref, kmin_ref, kmax_ref, last_ref, nfar_ref,
                 q_ref, k_ref, v_ref, qp_ref, kp_ref, tbl_ref, lam_ref, sub_ref,
                 o_ref, m_ref, l_ref, acc_ref, *, tk, tks, group, far_dist, lam_init):
    b = pl.program_id(0)
    i = pl.program_id(2)
    d = DA_HEAD_DIM
    tq = q_ref.shape[0]
    qmn = qmin_ref[b, i]
    qmx = qmax_ref[b, i]
    far_bias = tbl_ref[:, _LANES - 1:_LANES]

    m_ref[...] = jnp.full(m_ref.shape, _NEG, F32)
    l_ref[...] = jnp.zeros(l_ref.shape, F32)
    acc_ref[...] = jnp.zeros(acc_ref.shape, F32)

    def scores(r0, width):
        kblk = k_ref[pl.ds(r0, width), :]
        return [lax.dot_general(q_ref[:, mi * d:(mi + 1) * d], kblk[:, mi * d:(mi + 1) * d],
                                (((1,), (1,)), ((), ())), preferred_element_type=F32)
                for mi in range(2)]

    def update(r0, width, s, near):
        nrep = width // _LANES
        vblk = v_ref[pl.ds(r0, width), :]
        if near:
            n = qp_ref[...] - kp_ref[:, pl.ds(r0, width)]
            idx = jnp.clip(n, 0, _LANES - 1)
            table = jnp.broadcast_to(tbl_ref[...], (tq, _LANES))
            bias = jnp.concatenate(
                [jnp.take_along_axis(table, idx[:, c * _LANES:(c + 1) * _LANES], axis=1)
                 for c in range(nrep)], axis=1)
            keep = n >= 0
            s = [jnp.where(keep, sm + bias, _NEG) for sm in s]
            shift_bias = 0.0
        else:
            shift_bias = far_bias
        maps = range(2)
        m_prev = [m_ref[mi] for mi in maps]
        m_new = [jnp.maximum(m_prev[mi], jnp.max(s[mi], axis=-1, keepdims=True) + shift_bias)
                 for mi in maps]
        alpha = [jnp.exp2(m_prev[mi] - m_new[mi]) for mi in maps]
        p = [jnp.exp2(s[mi] - jnp.tile(m_new[mi] - shift_bias, (1, nrep))) for mi in maps]
        for mi in maps:
            psum = p[mi][:, :_LANES]
            for c in range(1, nrep):
                psum = psum + p[mi][:, c * _LANES:(c + 1) * _LANES]
            l_ref[mi] = alpha[mi] * l_ref[mi] + psum
            m_ref[mi] = m_new[mi]
        pv = [jnp.dot(p[mi].astype(BF16), vblk, preferred_element_type=F32) for mi in maps]
        for mi in maps:
            acc_ref[mi] = jnp.tile(alpha[mi], (1, 2 * d // _LANES)) * acc_ref[mi] + pv[mi]

    def far_group(j0, n):
        start = lambda u: pl.multiple_of((j0 + u) * tk, tk)
        s = scores(start(0), tk)
        for u in range(n):
            s_next = scores(start(u + 1), tk) if u + 1 < n else None
            update(start(u), tk, s, False)
            s = s_next

    def group_body(t, carry):
        far_group(group * t, group)
        return carry

    nfar = nfar_ref[b, i]
    ngroup = nfar // group
    lax.fori_loop(0, ngroup, group_body, 0)
    ngrouped = group * ngroup
    for half in (group // 2, group // 4):
        if half >= 1:
            take = (nfar - ngrouped) // half

            @pl.when(take == 1)
            def _(start=ngrouped, half=half):
                far_group(start, half)

            ngrouped = ngrouped + half * take

    def single_body(j, carry):
        r0 = pl.multiple_of(j * tks, tks)
        active = kmin_ref[b, j] <= qmx
        far = (qmn - kmax_ref[b, j]) >= far_dist

        @pl.when(jnp.logical_and(active, far))
        def _():
            update(r0, tks, scores(r0, tks), False)

        @pl.when(jnp.logical_and(active, jnp.logical_not(far)))
        def _():
            update(r0, tks, scores(r0, tks), True)

        return carry

    first = ngrouped * (tk // tks)
    stop = last_ref[b, i] + 1
    paired = (stop - first) >= 2
    single_stop = jnp.where(paired, stop - 2, stop)
    lax.fori_loop(first, single_stop, single_body, 0)

    @pl.when(paired)
    def _():
        r_a = pl.multiple_of((stop - 2) * tks, tks)
        r_b = pl.multiple_of((stop - 1) * tks, tks)
        s_a = scores(r_a, tks)
        s_b = scores(r_b, tks)
        update(r_a, tks, s_a, True)
        update(r_b, tks, s_b, True)

    lv = lam_ref[...]
    lam = (jnp.exp(jnp.sum(lv[0:1] * lv[1:2], axis=-1, keepdims=True))
           - jnp.exp(jnp.sum(lv[2:3] * lv[3:4], axis=-1, keepdims=True)) + lam_init)
    l1 = jnp.sum(l_ref[0], axis=-1, keepdims=True)
    l2 = jnp.sum(l_ref[1], axis=-1, keepdims=True)
    o = acc_ref[0] / l1 - lam * (acc_ref[1] / l2)
    o = o * lax.rsqrt(jnp.mean(o * o, axis=-1, keepdims=True) + LN_EPS)
    o_ref[...] = (o * sub_ref[...] * (1.0 - lam_init)).astype(o_ref.dtype)


def _diff_attention(qkv, positions, rel_bias, lam_vecs, subln_w, lam_init, *, tq=512, tk=1024, tks=512):
    B, S, D3 = qkv.shape
    D = D3 // 3
    H = D // (2 * DA_HEAD_DIM)
    w = 2 * DA_HEAD_DIM
    tq = _pick(S, (tq, 256, 128))
    tk = _pick(S, (tk, 256, 128))
    tks = min(tks, tk)
    nq, nk, nks = S // tq, S // tk, S // tks
    buckets = _t5_buckets(_LANES)
    assert buckets[-1] == N_BUCKETS - 1
    far_dist = buckets.index(N_BUCKETS - 1)
    table = rel_bias.astype(F32)[jnp.array(buckets, jnp.int32), :].T * _LOG2E
    table = table.reshape(H, 1, _LANES)

    pq = positions.reshape(B, nq, tq)
    pk = positions.reshape(B, nks, tks)
    qmin, qmax = pq.min(-1), pq.max(-1)
    kmin, kmax = pk.min(-1), pk.max(-1)
    needed = kmin[:, None, :] <= qmax[:, :, None]
    last = jnp.max(jnp.where(needed, jnp.arange(nks, dtype=jnp.int32), 0), axis=-1).astype(jnp.int32)
    kmax_wide = kmax.reshape(B, nk, tk // tks).max(-1)
    far = (qmin[:, :, None] - kmax_wide[:, None, :]) >= far_dist
    nfar = jnp.sum(jnp.cumprod(far.astype(jnp.int32), axis=-1), axis=-1).astype(jnp.int32)
    nfar = jnp.minimum(nfar, (last + 1) // (tk // tks))

    grid_spec = pltpu.PrefetchScalarGridSpec(
        num_scalar_prefetch=6,
        grid=(B, H, nq),
        in_specs=[
            pl.BlockSpec((None, tq, w), lambda b, h, i, *_: (b, i, h)),
            pl.BlockSpec((None, S, w), lambda b, h, i, *_: (b, 0, H + h)),
            pl.BlockSpec((None, S, w), lambda b, h, i, *_: (b, 0, 2 * H + h)),
            pl.BlockSpec((None, tq, 1), lambda b, h, i, *_: (b, i, 0)),
            pl.BlockSpec((None, 1, S), lambda b, h, i, *_: (b, 0, 0)),
            pl.BlockSpec((None, 1, _LANES), lambda b, h, i, *_: (h, 0, 0)),
            pl.BlockSpec((4, DA_HEAD_DIM), lambda b, h, i, *_: (0, 0)),
            pl.BlockSpec((1, w), lambda b, h, i, *_: (0, 0)),
        ],
        out_specs=pl.BlockSpec((None, tq, w), lambda b, h, i, *_: (b, i, h)),
        scratch_shapes=[pltpu.VMEM((2, tq, _LANES), F32), pltpu.VMEM((2, tq, _LANES), F32),
                        pltpu.VMEM((2, tq, w), F32)],
    )
    return pl.pallas_call(
        functools.partial(_attn_kernel, tk=tk, tks=tks, group=max(2, _ATTN_TRIP_KEYS // tk),
                          far_dist=far_dist, lam_init=lam_init),
        out_shape=jax.ShapeDtypeStruct((B, S, D), BF16),
        grid_spec=grid_spec,
        compiler_params=_params(("parallel", "parallel", "arbitrary")),
        name="diff_attention",
    )(qmin, qmax, kmin, kmax, last, nfar,
      qkv, qkv, qkv, positions.reshape(B, S, 1), positions.reshape(B, 1, S),
      table, lam_vecs.astype(F32), subln_w.astype(F32).reshape(1, w))


def _xattn_kernel(x_ref, h_ref, g0_ref, b0_ref, wq_ref, kv_ref, wo_ref, g_ref, b_ref,
                  o_ref, ob_ref, *, alpha):
    dh = CA_HEAD_DIM
    x1 = _layer_norm(alpha * x_ref[...] + h_ref[...], g0_ref[...], b0_ref[...])
    q = jnp.dot(x1.astype(BF16), wq_ref[...], preferred_element_type=F32) * (dh ** -0.5)
    q = q.astype(BF16)
    heads = []
    for hh in range(CA_HEADS):
        kh = kv_ref[:, hh * dh:(hh + 1) * dh]
        vh = kv_ref[:, CA_DIM + hh * dh:CA_DIM + (hh + 1) * dh]
        s = lax.dot_general(q[:, hh * dh:(hh + 1) * dh], kh, (((1,), (1,)), ((), ())),
                            preferred_element_type=F32)
        p = jnp.exp(s - jnp.max(s, axis=-1, keepdims=True))
        p = p / jnp.sum(p, axis=-1, keepdims=True)
        heads.append(jnp.dot(p.astype(BF16), vh, preferred_element_type=F32))
    o = jnp.concatenate(heads, axis=-1).astype(BF16)
    hcat = jnp.dot(o, wo_ref[...], preferred_element_type=F32)
    y = _layer_norm(alpha * x1 + hcat, g_ref[...], b_ref[...])
    o_ref[...] = y
    ob_ref[...] = y.astype(BF16)


def _norm_cross_attention(x, h, g0, b0, kv, w_q, w_o, g, b, alpha, *, B):
    M, D = x.shape
    S = M // B
    NM = kv.shape[1]
    tm = _pick(S, (256, 128))
    nt = S // tm
    row = pl.BlockSpec((tm, D), lambda i: (i, 0))
    vec = pl.BlockSpec((1, D), lambda i: (0, 0))
    return pl.pallas_call(
        functools.partial(_xattn_kernel, alpha=alpha),
        out_shape=(jax.ShapeDtypeStruct((M, D), F32), jax.ShapeDtypeStruct((M, D), BF16)),
        grid=(M // tm,),
        in_specs=[row, row, vec, vec,
                  pl.BlockSpec((D, CA_DIM), lambda i: (0, 0)),
                  pl.BlockSpec((None, NM, 2 * CA_DIM), lambda i: (i // nt, 0, 0)),
                  pl.BlockSpec((CA_DIM, D), lambda i: (0, 0)),
                  vec, vec],
        out_specs=(row, row),
        compiler_params=_params(("parallel",)),
        name="norm_cross_attention",
    )(x, h, g0.reshape(1, D), b0.reshape(1, D), w_q, kv, w_o, g.reshape(1, D), b.reshape(1, D))


def _ffn_up_kernel(x_ref, wg_ref, wv_ref, cwg_ref, cwv_ref, cbg_ref, cbv_ref, o_ref,
                   carry_ref, *, ts, tiles_per_seq):
    i = pl.program_id(0)
    j = pl.program_id(1)
    nsub = x_ref.shape[0] // ts
    row = lax.broadcasted_iota(jnp.int32, (ts, 1), 0)
    wg = wg_ref[...].astype(BF16)
    wv = wv_ref[...].astype(BF16)

    def up(u):
        x = x_ref[u * ts:(u + 1) * ts, :]
        return (jnp.dot(x, wg, preferred_element_type=F32), jnp.dot(x, wv, preferred_element_type=F32))

    def conv(hcur, prev, cw_ref, cb_ref):
        p1 = prev[7:8, :]
        p2 = prev[6:7, :]
        h1 = jnp.where(row == 0, p1, pltpu.roll(hcur, 1, axis=0))
        h2 = jnp.where(row == 0, p2, jnp.where(row == 1, p1, pltpu.roll(hcur, 2, axis=0)))
        return cw_ref[0:1, :] * h2 + cw_ref[1:2, :] * h1 + cw_ref[2:3, :] * hcur + cb_ref[...]

    prev = (carry_ref[0, j], carry_ref[1, j])
    raw = up(0)
    for u in range(nsub):
        raw_next = up(u + 1) if u + 1 < nsub else None
        seq_start = ((i * nsub + u) % tiles_per_seq) == 0
        gate = conv(raw[0], jnp.where(seq_start, 0.0, prev[0]), cwg_ref, cbg_ref)
        val = conv(raw[1], jnp.where(seq_start, 0.0, prev[1]), cwv_ref, cbv_ref)
        o_ref[u * ts:(u + 1) * ts, :] = (gate * jax.nn.sigmoid(gate) * val).astype(o_ref.dtype)
        prev = (raw[0][ts - 8:, :], raw[1][ts - 8:, :])
        raw = raw_next
    carry_ref[0, j] = prev[0]
    carry_ref[1, j] = prev[1]


def _ffn_up(xb, w_up_all, layer, conv_w, conv_b, *, S):
    M, D = xb.shape
    F2 = w_up_all.shape[2]
    F = F2 // 2
    tn = 256
    assert F % tn == 0
    nf = F // tn
    ts = _pick(S, (1024, 512, 256, 128))
    tm = 2 * ts if M % (2 * ts) == 0 else ts
    conv_b = conv_b.reshape(1, F2)
    return pl.pallas_call(
        functools.partial(_ffn_up_kernel, ts=ts, tiles_per_seq=S // ts),
        out_shape=jax.ShapeDtypeStruct((M, F), BF16),
        grid=(M // tm, nf),
        in_specs=[pl.BlockSpec((tm, D), lambda i, j: (i, 0), pipeline_mode=pl.Buffered(1)),
                  pl.BlockSpec((None, D, tn), lambda i, j: (layer, 0, j)),
                  pl.BlockSpec((None, D, tn), lambda i, j: (layer, 0, nf + j)),
                  pl.BlockSpec((CONV_W, tn), lambda i, j: (0, j)),
                  pl.BlockSpec((CONV_W, tn), lambda i, j: (0, nf + j)),
                  pl.BlockSpec((1, tn), lambda i, j: (0, j)),
                  pl.BlockSpec((1, tn), lambda i, j: (0, nf + j))],
        out_specs=pl.BlockSpec((tm, tn), lambda i, j: (i, j)),
        scratch_shapes=[pltpu.VMEM((2, nf, 8, tn), F32)],
        compiler_params=_params(("arbitrary", "arbitrary")),
        name="ffn_up_conv_glu",
    )(xb, w_up_all, w_up_all, conv_w, conv_w, conv_b, conv_b)


def _mix_kernel(x_ref, h_ref, g_ref, b_ref, mix_ref, xo_ref, *rest, alpha, tiles_per_seq):
    outs, carry_ref = rest[:-1], rest[-1]
    i = pl.program_id(0)
    tm = x_ref.shape[0]
    x = _layer_norm(alpha * x_ref[...] + h_ref[...], g_ref[...], b_ref[...])
    xo_ref[...] = x
    row = lax.broadcasted_iota(jnp.int32, (tm, 1), 0)
    prev_last = jnp.where((i % tiles_per_seq) == 0, 0.0, carry_ref[7:8, :])
    xx = jnp.where(row == 0, prev_last, pltpu.roll(x, 1, axis=0)) - x
    carry_ref[...] = x[tm - 8:, :]
    for m, o_ref in enumerate(outs):
        o_ref[...] = (x + xx * mix_ref[m:m + 1, :]).astype(BF16)


def _norm_token_shift_mix(x, h, g, b, mix, alpha, *, S):
    M, D = x.shape
    tm = _pick(S, (128,))
    n = mix.shape[0]
    row = pl.BlockSpec((tm, D), lambda i: (i, 0))
    vec = pl.BlockSpec((1, D), lambda i: (0, 0))
    outs = pl.pallas_call(
        functools.partial(_mix_kernel, alpha=alpha, tiles_per_seq=S // tm),
        out_shape=(jax.ShapeDtypeStruct((M, D), F32),)
        + tuple(jax.ShapeDtypeStruct((M, D), BF16) for _ in range(n)),
        grid=(M // tm,),
        in_specs=[row, row, vec, vec, pl.BlockSpec((n, D), lambda i: (0, 0))],
        out_specs=(row,) + tuple(row for _ in range(n)),
        scratch_shapes=[pltpu.VMEM((8, D), F32)],
        compiler_params=_params(("arbitrary",)),
        name="norm_token_shift_mix",
    )(x, h, g.reshape(1, D), b.reshape(1, D), mix)
    return outs[0], outs[1:]


def _lora_kernel(x_ref, w1_ref, w2_ref, b_ref, o_ref, *, mode):
    t = jnp.dot(x_ref[...], w1_ref[...], preferred_element_type=F32)
    if mode == "decay":
        t = jnp.tanh(t)
    elif mode == "gate":
        t = jax.nn.sigmoid(t)
    z = jnp.dot(t.astype(BF16), w2_ref[...], preferred_element_type=F32) + b_ref[...]
    if mode == "decay":
        u = -z
        sp = jnp.maximum(u, 0.0) + jnp.log(1.0 + jnp.exp(-jnp.abs(u)))
        z = -jnp.exp(-sp - 0.5)
    elif mode == "lr":
        z = jax.nn.sigmoid(z)
    o_ref[...] = z


def _lora(xb, w1, w2, bias, mode):
    M, D = xb.shape
    R = w1.shape[1]
    tm = _pick(M, (512, 256, 128))
    row = pl.BlockSpec((tm, D), lambda i: (i, 0))
    return pl.pallas_call(
        functools.partial(_lora_kernel, mode=mode),
        out_shape=jax.ShapeDtypeStruct((M, D), F32),
        grid=(M // tm,),
        in_specs=[row, pl.BlockSpec((D, R), lambda i: (0, 0)),
                  pl.BlockSpec((R, D), lambda i: (0, 0)),
                  pl.BlockSpec((1, D), lambda i: (0, 0))],
        out_specs=row,
        compiler_params=_params(("parallel",)),
        name="lora_" + mode,
    )(xb, w1, w2, bias.reshape(1, D))


def _dot_bf(a, b):
    return jnp.dot(a.astype(BF16), b.astype(BF16), preferred_element_type=F32)


def _split3(x):
    hi = x.astype(BF16)
    r1 = x - hi.astype(F32)
    mid = r1.astype(BF16)
    lo = (r1 - mid.astype(F32)).astype(BF16)
    return hi, mid, lo


def _dot_exact_rhs(x, w_bf):
    n = x.shape[0]
    r = jnp.dot(jnp.concatenate(_split3(x), axis=0), w_bf, preferred_element_type=F32)
    return r[:n] + r[n:2 * n] + r[2 * n:]


def _dot_exact_lhs(w_bf, x):
    n = x.shape[1]
    r = jnp.dot(w_bf, jnp.concatenate(_split3(x), axis=1), preferred_element_type=F32)
    return r[:, :n] + r[:, n:2 * n] + r[:, 2 * n:]


def _dot_x3(a, b):
    n = a.shape[0]
    a_hi = a.astype(BF16)
    a_lo = (a - a_hi.astype(F32)).astype(BF16)
    b_hi = b.astype(BF16)
    b_lo = (b - b_hi.astype(F32)).astype(BF16)
    r = jnp.dot(jnp.concatenate([a_hi, a_lo], axis=0), b_hi, preferred_element_type=F32)
    return r[:n] + r[n:] + jnp.dot(a_hi, b_lo, preferred_element_type=F32)


def _head_ones():
    r = lax.broadcasted_iota(jnp.int32, (_LANES, _LANES), 0) // RW_HEAD
    c = lax.broadcasted_iota(jnp.int32, (_LANES, _LANES), 1) // RW_HEAD
    return jnp.where(r == c, 1.0, 0.0).astype(BF16)


def _rwkv_prep_kernel(r_ref, k_ref, v_ref, lw_ref, a_ref, kk_ref, ka_ref, rk_ref, lb_ref,
                      rp_ref, y0_ref, e_ref, g_ref, h_ref, *, npairs):
    L = RW_CHUNK
    n2 = 2 * L
    pairs = range(npairs)
    sls = [slice(p * _LANES, (p + 1) * _LANES) for p in pairs]
    ones_bd = _head_ones()
    ri = lax.broadcasted_iota(jnp.int32, (n2, n2), 0)
    ci = lax.broadcasted_iota(jnp.int32, (n2, n2), 1)
    strict = ci < ri
    incl = ci <= ri
    diag = ri == ci
    tri = jnp.where(lax.broadcasted_iota(jnp.int32, (L, L), 1)
                    <= lax.broadcasted_iota(jnp.int32, (L, L), 0), 1.0, 0.0).astype(BF16)
    head0 = lax.broadcasted_iota(jnp.int32, (L, _LANES), 1) < RW_HEAD

    def stack(x):
        return jnp.concatenate([jnp.where(head0, x, 0.0), jnp.where(head0, 0.0, x)], axis=0)

    def unstack(x):
        return x[:L] + x[L:]

    r = [r_ref[:, sl] for sl in sls]
    k = [k_ref[:, sl] for sl in sls]
    v = [v_ref[:, sl] for sl in sls]
    lw = [lw_ref[:, sl] for sl in sls]
    a = [a_ref[:, sl] for sl in sls]
    kk = [k[p] * kk_ref[:, sls[p]] for p in pairs]
    kmod = [k[p] * (1.0 + (a[p] - 1.0) * ka_ref[:, sls[p]]) for p in pairs]
    cl = [_dot_exact_lhs(tri, lw[p]) for p in pairs]
    hs = [_dot_exact_rhs(jnp.concatenate([kk[p] * kk[p], r[p] * kmod[p] * rk_ref[:, sls[p]]], axis=0),
                         ones_bd) for p in pairs]
    for p in pairs:
        e_ref[:, sls[p]] = lb_ref[:, sls[p]] + hs[p][L:] * v[p]
    kk = [kk[p] / jnp.maximum(jnp.sqrt(hs[p][:L]), 1e-12) for p in pairs]
    bv = [kk[p] * a[p] for p in pairs]
    cl_end = [c[L - 1:L, :] for c in cl]
    c_inv = [jnp.exp(-c) for c in cl]
    c_rem = [jnp.exp(cl_end[p] - cl[p]) for p in pairs]
    a_s = [stack(-kk[p] * jnp.exp(cl[p] - lw[p])) for p in pairs]
    r_s = [stack(r[p] * jnp.exp(cl[p])) for p in pairs]
    v_s = [stack(v[p]).astype(BF16) for p in pairs]
    bk_t = [jnp.concatenate([stack(bv[p] * c_inv[p]), stack(kmod[p] * c_inv[p])], axis=0)
            .astype(BF16).T for p in pairs]
    ar_bk = [_dot_bf(jnp.concatenate([a_s[p], r_s[p]], axis=0), bk_t[p]) for p in pairs]
    nab = [jnp.where(strict, m[:n2, :n2], 0.0) for m in ar_bk]
    nak = [jnp.where(strict, m[:n2, n2:], 0.0) for m in ar_bk]
    mrb = [jnp.where(incl, m[n2:, :n2], 0.0) for m in ar_bk]
    mrk = [jnp.where(incl, m[n2:, n2:], 0.0) for m in ar_bk]
    xv = [_dot_bf(jnp.concatenate([nak[p], mrk[p], stack(kmod[p] * c_rem[p]).T], axis=0), v_s[p])
          for p in pairs]
    levels = int(math.log2(L)) - 1
    t_inv = [jnp.where(diag, 1.0, m) for m in nab]
    x = [_dot_bf(m, m) for m in nab]
    for lvl in range(levels):
        if lvl < levels - 1:
            tx = [_dot_bf(jnp.concatenate([t_inv[p], x[p]], axis=0), x[p]) for p in pairs]
            t_inv = [t_inv[p] + tx[p][:n2] for p in pairs]
            x = [tx[p][n2:] for p in pairs]
        else:
            t_inv = [t_inv[p] + _dot_bf(t_inv[p], x[p]) for p in pairs]
    pq = [_dot_bf(t_inv[p], jnp.concatenate([a_s[p], xv[p][:n2]], axis=1)) for p in pairs]
    rg = [_dot_bf(jnp.concatenate([mrb[p], stack(bv[p] * c_rem[p]).T], axis=0), pq[p]) for p in pairs]
    for p in pairs:
        rp_ref[:, sls[p]] = unstack(r_s[p] + rg[p][:n2, :_LANES])
        y0_ref[:, sls[p]] = unstack(rg[p][:n2, _LANES:] + xv[p][n2:2 * n2])
        g_ref[p] = unstack(jnp.where(diag, jnp.exp(cl_end[p]), 0.0) + rg[p][n2:, :_LANES])
        h_ref[p] = unstack(rg[p][n2:, _LANES:] + xv[p][2 * n2:])


def _rwkv_scan_kernel(rp_ref, y0_ref, e_ref, gate_ref, g_ref, h_ref, lnw_ref, o_ref, st_ref, *, npairs):
    c = pl.program_id(2)

    @pl.when(c == 0)
    def _():
        st_ref[...] = jnp.zeros(st_ref.shape, F32)

    ones_bd = _head_ones()
    inv_n = 1.0 / RW_HEAD
    pairs = range(npairs)
    sls = [slice(p * _LANES, (p + 1) * _LANES) for p in pairs]
    head0 = lax.broadcasted_iota(jnp.int32, (RW_HEAD, _LANES), 1) < RW_HEAD

    def block_diag(x):
        return jnp.concatenate([jnp.where(head0, x, 0.0), jnp.where(head0, 0.0, x)], axis=0)

    prod = [_dot_x3(jnp.concatenate([block_diag(g_ref[p]), rp_ref[:, sls[p]]], axis=0), st_ref[p])
            for p in pairs]
    for p in pairs:
        st_ref[p] = prod[p][:_LANES] + block_diag(h_ref[p])
    y = [prod[p][_LANES:] + y0_ref[:, sls[p]] for p in pairs]
    mu = [_dot_exact_rhs(y[p], ones_bd) * inv_n for p in pairs]
    yc = [y[p] - mu[p] for p in pairs]
    var = [_dot_exact_rhs(yc[p] * yc[p], ones_bd) * inv_n for p in pairs]
    for p in pairs:
        yn = yc[p] * lax.rsqrt(var[p] + GN_EPS)
        o_ref[:, sls[p]] = ((yn * lnw_ref[:, sls[p]] + e_ref[:, sls[p]])
                            * gate_ref[:, sls[p]]).astype(o_ref.dtype)


def _rwkv_core(r, k, v, lw, a, gate, k_k, k_a, r_k, lnx_w, lnx_b, *, B):
    M, D = r.shape
    S = M // B
    L = RW_CHUNK
    nc = S // L
    ngroups = D // _LANES
    np1 = _pick(ngroups, (8, 4, 2, 1))
    np2 = _pick(ngroups, (16, 8, 4, 2, 1))
    vec = lambda t: t.astype(F32).reshape(1, D)

    w1 = np1 * _LANES
    tile1 = pl.BlockSpec((L, w1), lambda b, c, g: (b * nc + c, g))
    par1 = pl.BlockSpec((1, w1), lambda b, c, g: (0, g))
    mat1 = pl.BlockSpec((None, None, np1, RW_HEAD, _LANES), lambda b, c, g: (b, c, g, 0, 0))
    md = jax.ShapeDtypeStruct((M, D), F32)
    gh = jax.ShapeDtypeStruct((B, nc, ngroups, RW_HEAD, _LANES), F32)
    rp, y0, e, gm, hm = pl.pallas_call(
        functools.partial(_rwkv_prep_kernel, npairs=np1),
        out_shape=(md, md, md, gh, gh),
        grid=(B, nc, ngroups // np1),
        in_specs=[tile1] * 5 + [par1] * 4,
        out_specs=(tile1, tile1, tile1, mat1, mat1),
        compiler_params=_params(("parallel", "parallel", "parallel")),
        name="rwkv_chunk_prep",
    )(r, k, v, lw, a, vec(k_k), vec(k_a), vec(r_k), vec(lnx_b))

    w2 = np2 * _LANES
    tile2 = pl.BlockSpec((L, w2), lambda b, g, c: (b * nc + c, g))
    par2 = pl.BlockSpec((1, w2), lambda b, g, c: (0, g))
    mat2 = pl.BlockSpec((None, None, np2, RW_HEAD, _LANES), lambda b, g, c: (b, c, g, 0, 0))
    return pl.pallas_call(
        functools.partial(_rwkv_scan_kernel, npairs=np2),
        out_shape=jax.ShapeDtypeStruct((M, D), BF16),
        grid=(B, ngroups // np2, nc),
        in_specs=[tile2] * 4 + [mat2, mat2, par2],
        out_specs=tile2,
        scratch_shapes=[pltpu.VMEM((np2, _LANES, _LANES), F32)],
        compiler_params=_params(("parallel", "parallel", "arbitrary")),
        name="rwkv_state_scan",
    )(rp, y0, e, gate, gm, hm, vec(lnx_w))


def kernel(x, mem, positions, rel_bias, da_w_qkv, da_lam, da_subln, da_w_o, rw_mix, rw_w_rkv, rw_w0, rw_w1, rw_w2, rw_a0, rw_a1, rw_a2, rw_g1, rw_g2, rw_k_k, rw_k_a, rw_r_k, rw_lnx_w, rw_lnx_b, rw_w_o, ca_w_q, ca_w_kv, ca_w_o, ffn_w_up, ffn_conv_w, ffn_conv_b, ffn_w_down, ln_g, ln_b):
    B, S, D = x.shape
    M = B * S
    depth = ln_g.shape[0]
    alpha = (2 * depth) ** 0.25
    bf = lambda t: t.astype(BF16)
    memb = bf(mem).reshape(B * mem.shape[1], D)
    F = ffn_w_down.shape[1]
    tk_down = F // 2 if (F // 2) % _LANES == 0 else F

    xf = x.reshape(M, D)
    pending = None
    for i in range(depth):
        j = i // 2
        if i % 2 == 0:
            if pending is None:
                xb = bf(xf)
            else:
                xf, xb = _deepnorm(xf, *pending, alpha, with_bf16=True)
            lam_init = 0.8 - 0.6 * math.exp(-0.3 * i)
            qkv = _matmul_f32w(xb, da_w_qkv, (j,), BF16, scaled_cols=D,
                               col_scale=DA_HEAD_DIM ** -0.5 * _LOG2E)
            att = _diff_attention(qkv.reshape(B, S, 3 * D), positions, rel_bias, da_lam[j],
                                  da_subln[j], lam_init)
            h = _matmul_f32w(att.reshape(M, D), da_w_o, (j,), F32)
        else:
            xf, (xr, xw, xk, xv, xa, xg) = _norm_token_shift_mix(
                xf, *pending, rw_mix[j].astype(F32), alpha, S=S)
            r = _matmul_f32w(xr, rw_w_rkv, (j, 0), F32)
            k = _matmul_f32w(xk, rw_w_rkv, (j, 1), F32)
            v = _matmul_f32w(xv, rw_w_rkv, (j, 2), F32)
            lw = _lora(xw, bf(rw_w1[j]), bf(rw_w2[j]), rw_w0[j].astype(F32), "decay")
            a = _lora(xa, bf(rw_a1[j]), bf(rw_a2[j]), rw_a0[j].astype(F32), "lr")
            g = _lora(xg, bf(rw_g1[j]), bf(rw_g2[j]), jnp.zeros((D,), F32), "gate")
            o = _rwkv_core(r, k, v, lw, a, g, rw_k_k[j], rw_k_a[j], rw_r_k[j].reshape(D),
                           rw_lnx_w[j], rw_lnx_b[j], B=B)
            h = _matmul_f32w(o, rw_w_o, (j,), F32)

        kv = _matmul(memb, bf(ca_w_kv[i]), BF16).reshape(B, mem.shape[1], 2 * CA_DIM)
        xf, xb = _norm_cross_attention(xf, h, ln_g[i, 0], ln_b[i, 0], kv, bf(ca_w_q[i]),
                                       bf(ca_w_o[i]), ln_g[i, 1], ln_b[i, 1], alpha, B=B)

        act = _ffn_up(xb, ffn_w_up.astype(F32), i, ffn_conv_w[i].astype(F32), ffn_conv_b[i].astype(F32), S=S)
        h = _matmul(act, bf(ffn_w_down[i]), F32, tn=512, tk=tk_down)
        pending = (h, ln_g[i, 2], ln_b[i, 2])
    (xf,) = _deepnorm(xf, *pending, alpha, with_bf16=False)
    return xf.reshape(B, S, D)
```

```python
import functools
import math

import jax
import jax.numpy as jnp
from jax import lax
from jax.experimental import pallas as pl
from jax.experimental.pallas import tpu as pltpu

F32 = jnp.float32
BF16 = jnp.bfloat16

_VMEM_LIMIT_BYTES = 56 * 1024 * 1024
_LANES = 128

DA_HEAD_DIM = 128
N_BUCKETS = 32
MAX_DIST = 128
RW_HEAD = 64
GN_EPS = 64e-5
CA_HEADS = 4
CA_HEAD_DIM = 128
CA_DIM = CA_HEADS * CA_HEAD_DIM
CONV_W = 3
LN_EPS = 1e-5
RW_CHUNK = 64

_ATTN_TRIP_KEYS = 2048
_NEG = float(jnp.finfo(jnp.float32).min)
_LOG2E = math.log2(math.e)


def _params(sem):
    return pltpu.CompilerParams(dimension_semantics=sem, vmem_limit_bytes=_VMEM_LIMIT_BYTES)


def _pick(n, prefs):
    for p in prefs:
        if n % p == 0:
            return p
    return n


def _mm_kernel(x_ref, w_ref, o_ref, *scratch, nk, scaled_tiles, col_scale):
    part = jnp.dot(x_ref[...], w_ref[...], preferred_element_type=F32)

    def finish(acc):
        if scaled_tiles:
            acc = acc * jnp.where(pl.program_id(1) < scaled_tiles, col_scale, 1.0)
        o_ref[...] = acc.astype(o_ref.dtype)

    if nk == 1:
        finish(part)
        return
    (acc_ref,) = scratch
    k = pl.program_id(2)

    @pl.when(k == 0)
    def _():
        acc_ref[...] = part

    @pl.when(jnp.logical_and(k > 0, k < nk - 1))
    def _():
        acc_ref[...] += part

    @pl.when(k == nk - 1)
    def _():
        finish(acc_ref[...] + part)


def _matmul(x, w, out_dtype, *, tm=1024, tn=1024, tk=None, scaled_cols=0, col_scale=1.0):
    M, K = x.shape
    N = w.shape[1]
    tm = _pick(M, (tm, 512, 256, 128))
    tn = _pick(N, (tn, 512, 256, 128))
    tk = K if tk is None else tk
    nk = K // tk
    assert K % tk == 0 and scaled_cols % tn == 0
    scratch = [pltpu.VMEM((tm, tn), F32)] if nk > 1 else []
    return pl.pallas_call(
        functools.partial(_mm_kernel, nk=nk, scaled_tiles=scaled_cols // tn, col_scale=col_scale),
        out_shape=jax.ShapeDtypeStruct((M, N), out_dtype),
        grid=(M // tm, N // tn, nk),
        in_specs=[pl.BlockSpec((tm, tk), lambda i, j, k: (i, k)),
                  pl.BlockSpec((tk, tn), lambda i, j, k: (k, j))],
        out_specs=pl.BlockSpec((tm, tn), lambda i, j, k: (i, j)),
        scratch_shapes=scratch,
        compiler_params=_params(("parallel", "parallel", "arbitrary")),
        name="matmul",
    )(x, w)


def _mm_f32w_kernel(x_ref, w_ref, o_ref, *, scaled_tiles, col_scale):
    acc = jnp.dot(x_ref[...], w_ref[...].astype(BF16), preferred_element_type=F32)
    if scaled_tiles:
        acc = acc * jnp.where(pl.program_id(1) < scaled_tiles, col_scale, 1.0)
    o_ref[...] = acc.astype(o_ref.dtype)


def _matmul_f32w(x, w_all, lead, out_dtype, *, tm=1024, tn=512, scaled_cols=0, col_scale=1.0):
    M, K = x.shape
    N = w_all.shape[-1]
    tm = _pick(M, (tm, 512, 256, 128))
    tn = _pick(N, (tn, 256, 128))
    assert scaled_cols % tn == 0 and len(lead) == w_all.ndim - 2
    return pl.pallas_call(
        functools.partial(_mm_f32w_kernel, scaled_tiles=scaled_cols // tn, col_scale=col_scale),
        out_shape=jax.ShapeDtypeStruct((M, N), out_dtype),
        grid=(M // tm, N // tn),
        in_specs=[pl.BlockSpec((tm, K), lambda i, j: (i, 0)),
                  pl.BlockSpec((None,) * len(lead) + (K, tn), lambda i, j: tuple(lead) + (0, j))],
        out_specs=pl.BlockSpec((tm, tn), lambda i, j: (i, j)),
        compiler_params=_params(("parallel", "arbitrary")),
        name="matmul_f32w",
    )(x, w_all.astype(F32))


def _layer_norm(z, g, b):
    mu = jnp.mean(z, axis=-1, keepdims=True)
    zc = z - mu
    var = jnp.mean(zc * zc, axis=-1, keepdims=True)
    return zc * lax.rsqrt(var + LN_EPS) * g + b


def _ln_kernel(x_ref, h_ref, g_ref, b_ref, o_ref, *bf16_out, alpha):
    y = _layer_norm(alpha * x_ref[...] + h_ref[...], g_ref[...], b_ref[...])
    o_ref[...] = y
    for ob_ref in bf16_out:
        ob_ref[...] = y.astype(BF16)


def _deepnorm(x, h, g, b, alpha, *, with_bf16):
    M, D = x.shape
    tm = _pick(M, (256, 128))
    row = pl.BlockSpec((tm, D), lambda i: (i, 0))
    vec = pl.BlockSpec((1, D), lambda i: (0, 0))
    shapes = (jax.ShapeDtypeStruct((M, D), F32),) + ((jax.ShapeDtypeStruct((M, D), BF16),) if with_bf16 else ())
    return pl.pallas_call(
        functools.partial(_ln_kernel, alpha=alpha),
        out_shape=shapes,
        grid=(M // tm,),
        in_specs=[row, row, vec, vec],
        out_specs=tuple(row for _ in shapes),
        compiler_params=_params(("parallel",)),
        name="deepnorm",
    )(x, h, g.reshape(1, D), b.reshape(1, D))


def _t5_buckets(n_dist):
    max_exact = N_BUCKETS // 2
    out = []
    for n in range(n_dist):
        if n < max_exact:
            out.append(n)
            continue
        val = math.log(n / max_exact) / math.log(MAX_DIST / max_exact) * (N_BUCKETS - max_exact)
        assert n == max_exact or n >= MAX_DIST or abs(val - round(val)) > 1e-3
        out.append(min(max_exact + int(val), N_BUCKETS - 1))
    return out


def _attn_kernel(qmin_ref, qmax_ref, kmin_ref, kmax_ref, last_ref, nfar_ref,
                 q_ref, k_ref, v_ref, qp_ref, kp_ref, tbl_ref, lam_ref, sub_ref,
                 o_ref, m_ref, l_ref, acc_ref, *, tk, tks, group, far_dist, lam_init):
    b = pl.program_id(0)
    i = pl.program_id(2)
    d = DA_HEAD_DIM
    tq = q_ref.shape[0]
    qmn = qmin_ref[b, i]
    qmx = qmax_ref[b, i]
    far_bias = tbl_ref[:, _LANES - 1:_LANES]

    m_ref[...] = jnp.full(m_ref.shape, _NEG, F32)
    l_ref[...] = jnp.zeros(l_ref.shape, F32)
    acc_ref[...] = jnp.zeros(acc_ref.shape, F32)

    def scores(r0, width):
        kblk = k_ref[pl.ds(r0, width), :]
        return [lax.dot_general(q_ref[:, mi * d:(mi + 1) * d], kblk[:, mi * d:(mi + 1) * d],
                                (((1,), (1,)), ((), ())), preferred_element_type=F32)
                for mi in range(2)]

    def update(r0, width, s, near):
        nrep = width // _LANES
        vblk = v_ref[pl.ds(r0, width), :]
        if near:
            n = qp_ref[...] - kp_ref[:, pl.ds(r0, width)]
            idx = jnp.clip(n, 0, _LANES - 1)
            table = jnp.broadcast_to(tbl_ref[...], (tq, _LANES))
            bias = jnp.concatenate(
                [jnp.take_along_axis(table, idx[:, c * _LANES:(c + 1) * _LANES], axis=1)
                 for c in range(nrep)], axis=1)
            keep = n >= 0
            s = [jnp.where(keep, sm + bias, _NEG) for sm in s]
            shift_bias = 0.0
        else:
            shift_bias = far_bias
        maps = range(2)
        m_prev = [m_ref[mi] for mi in maps]
        m_new = [jnp.maximum(m_prev[mi], jnp.max(s[mi], axis=-1, keepdims=True) + shift_bias)
                 for mi in maps]
        alpha = [jnp.exp2(m_prev[mi] - m_new[mi]) for mi in maps]
        p = [jnp.exp2(s[mi] - jnp.tile(m_new[mi] - shift_bias, (1, nrep))) for mi in maps]
        for mi in maps:
            psum = p[mi][:, :_LANES]
            for c in range(1, nrep):
                psum = psum + p[mi][:, c * _LANES:(c + 1) * _LANES]
            l_ref[mi] = alpha[mi] * l_ref[mi] + psum
            m_ref[mi] = m_new[mi]
        pv = [jnp.dot(p[mi].astype(BF16), vblk, preferred_element_type=F32) for mi in maps]
        for mi in maps:
            acc_ref[mi] = jnp.tile(alpha[mi], (1, 2 * d // _LANES)) * acc_ref[mi] + pv[mi]

    def far_group(j0, n):
        start = lambda u: pl.multiple_of((j0 + u) * tk, tk)
        s = scores(start(0), tk)
        for u in range(n):
            s_next = scores(start(u + 1), tk) if u + 1 < n else None
            update(start(u), tk, s, False)
            s = s_next

    def group_body(t, carry):
        far_group(group * t, group)
        return carry

    nfar = nfar_ref[b, i]
    ngroup = nfar // group
    lax.fori_loop(0, ngroup, group_body, 0)
    ngrouped = group * ngroup
    for half in (group // 2, group // 4):
        if half >= 1:
            take = (nfar - ngrouped) // half

            @pl.when(take == 1)
            def _(start=ngrouped, half=half):
                far_group(start, half)

            ngrouped = ngrouped + half * take

    def single_body(j, carry):
        r0 = pl.multiple_of(j * tks, tks)
        active = kmin_ref[b, j] <= qmx
        far = (qmn - kmax_ref[b, j]) >= far_dist

        @pl.when(jnp.logical_and(active, far))
        def _():
            update(r0, tks, scores(r0, tks), False)

        @pl.when(jnp.logical_and(active, jnp.logical_not(far)))
        def _():
            update(r0, tks, scores(r0, tks), True)

        return carry

    first = ngrouped * (tk // tks)
    stop = last_ref[b, i] + 1
    paired = (stop - first) >= 2
    single_stop = jnp.where(paired, stop - 2, stop)
    lax.fori_loop(first, single_stop, single_body, 0)

    @pl.when(paired)
    def _():
        r_a = pl.multiple_of((stop - 2) * tks, tks)
        r_b = pl.multiple_of((stop - 1) * tks, tks)
        s_a = scores(r_a, tks)
        s_b = scores(r_b, tks)
        update(r_a, tks, s_a, True)
        update(r_b, tks, s_b, True)

    lv = lam_ref[...]
    lam = (jnp.exp(jnp.sum(lv[0:1] * lv[1:2], axis=-1, keepdims=True))
           - jnp.exp(jnp.sum(lv[2:3] * lv[3:4], axis=-1, keepdims=True)) + lam_init)
    l1 = jnp.sum(l_ref[0], axis=-1, keepdims=True)
    l2 = jnp.sum(l_ref[1], axis=-1, keepdims=True)
    o = acc_ref[0] / l1 - lam * (acc_ref[1] / l2)
    o = o * lax.rsqrt(jnp.mean(o * o, axis=-1, keepdims=True) + LN_EPS)
    o_ref[...] = (o * sub_ref[...] * (1.0 - lam_init)).astype(o_ref.dtype)


def _diff_attention(qkv, positions, rel_bias, lam_vecs, subln_w, lam_init, *, tq=512, tk=1024, tks=512):
    B, S, D3 = qkv.shape
    D = D3 // 3
    H = D // (2 * DA_HEAD_DIM)
    w = 2 * DA_HEAD_DIM
    tq = _pick(S, (tq, 256, 128))
    tk = _pick(S, (tk, 256, 128))
    tks = min(tks, tk)
    nq, nk, nks = S // tq, S // tk, S // tks
    buckets = _t5_buckets(_LANES)
    assert buckets[-1] == N_BUCKETS - 1
    far_dist = buckets.index(N_BUCKETS - 1)
    table = rel_bias.astype(F32)[jnp.array(buckets, jnp.int32), :].T * _LOG2E
    table = table.reshape(H, 1, _LANES)

    pq = positions.reshape(B, nq, tq)
    pk = positions.reshape(B, nks, tks)
    qmin, qmax = pq.min(-1), pq.max(-1)
    kmin, kmax = pk.min(-1), pk.max(-1)
    needed = kmin[:, None, :] <= qmax[:, :, None]
    last = jnp.max(jnp.where(needed, jnp.arange(nks, dtype=jnp.int32), 0), axis=-1).astype(jnp.int32)
    kmax_wide = kmax.reshape(B, nk, tk // tks).max(-1)
    far = (qmin[:, :, None] - kmax_wide[:, None, :]) >= far_dist
    nfar = jnp.sum(jnp.cumprod(far.astype(jnp.int32), axis=-1), axis=-1).astype(jnp.int32)
    nfar = jnp.minimum(nfar, (last + 1) // (tk // tks))

    grid_spec = pltpu.PrefetchScalarGridSpec(
        num_scalar_prefetch=6,
        grid=(B, H, nq),
        in_specs=[
            pl.BlockSpec((None, tq, w), lambda b, h, i, *_: (b, i, h)),
            pl.BlockSpec((None, S, w), lambda b, h, i, *_: (b, 0, H + h)),
            pl.BlockSpec((None, S, w), lambda b, h, i, *_: (b, 0, 2 * H + h)),
            pl.BlockSpec((None, tq, 1), lambda b, h, i, *_: (b, i, 0)),
            pl.BlockSpec((None, 1, S), lambda b, h, i, *_: (b, 0, 0)),
            pl.BlockSpec((None, 1, _LANES), lambda b, h, i, *_: (h, 0, 0)),
            pl.BlockSpec((4, DA_HEAD_DIM), lambda b, h, i, *_: (0, 0)),
            pl.BlockSpec((1, w), lambda b, h, i, *_: (0, 0)),
        ],
        out_specs=pl.BlockSpec((None, tq, w), lambda b, h, i, *_: (b, i, h)),
        scratch_shapes=[pltpu.VMEM((2, tq, _LANES), F32), pltpu.VMEM((2, tq, _LANES), F32),
                        pltpu.VMEM((2, tq, w), F32)],
    )
    return pl.pallas_call(
        functools.partial(_attn_kernel, tk=tk, tks=tks, group=max(2, _ATTN_TRIP_KEYS // tk),
                          far_dist=far_dist, lam_init=lam_init),
        out_shape=jax.ShapeDtypeStruct((B, S, D), BF16),
        grid_spec=grid_spec,
        compiler_params=_params(("parallel", "parallel", "arbitrary")),
        name="diff_attention",
    )(qmin, qmax, kmin, kmax, last, nfar,
      qkv, qkv, qkv, positions.reshape(B, S, 1), positions.reshape(B, 1, S),
      table, lam_vecs.astype(F32), subln_w.astype(F32).reshape(1, w))


def _xattn_kernel(x_ref, h_ref, g0_ref, b0_ref, wq_ref, kv_ref, wo_ref, g_ref, b_ref,
                  o_ref, ob_ref, *, alpha):
    dh = CA_HEAD_DIM
    x1 = _layer_norm(alpha * x_ref[...] + h_ref[...], g0_ref[...], b0_ref[...])
    q = jnp.dot(x1.astype(BF16), wq_ref[...], preferred_element_type=F32) * (dh ** -0.5)
    q = q.astype(BF16)
    heads = []
    for hh in range(CA_HEADS):
        kh = kv_ref[:, hh * dh:(hh + 1) * dh]
        vh = kv_ref[:, CA_DIM + hh * dh:CA_DIM + (hh + 1) * dh]
        s = lax.dot_general(q[:, hh * dh:(hh + 1) * dh], kh, (((1,), (1,)), ((), ())),
                            preferred_element_type=F32)
        p = jnp.exp(s - jnp.max(s, axis=-1, keepdims=True))
        p = p / jnp.sum(p, axis=-1, keepdims=True)
        heads.append(jnp.dot(p.astype(BF16), vh, preferred_element_type=F32))
    o = jnp.concatenate(heads, axis=-1).astype(BF16)
    hcat = jnp.dot(o, wo_ref[...], preferred_element_type=F32)
    y = _layer_norm(alpha * x1 + hcat, g_ref[...], b_ref[...])
    o_ref[...] = y
    ob_ref[...] = y.astype(BF16)


def _norm_cross_attention(x, h, g0, b0, kv, w_q, w_o, g, b, alpha, *, B):
    M, D = x.shape
    S = M // B
    NM = kv.shape[1]
    tm = _pick(S, (256, 128))
    nt = S // tm
    row = pl.BlockSpec((tm, D), lambda i: (i, 0))
    vec = pl.BlockSpec((1, D), lambda i: (0, 0))
    return pl.pallas_call(
        functools.partial(_xattn_kernel, alpha=alpha),
        out_shape=(jax.ShapeDtypeStruct((M, D), F32), jax.ShapeDtypeStruct((M, D), BF16)),
        grid=(M // tm,),
        in_specs=[row, row, vec, vec,
                  pl.BlockSpec((D, CA_DIM), lambda i: (0, 0)),
                  pl.BlockSpec((None, NM, 2 * CA_DIM), lambda i: (i // nt, 0, 0)),
                  pl.BlockSpec((CA_DIM, D), lambda i: (0, 0)),
                  vec, vec],
        out_specs=(row, row),
        compiler_params=_params(("parallel",)),
        name="norm_cross_attention",
    )(x, h, g0.reshape(1, D), b0.reshape(1, D), w_q, kv, w_o, g.reshape(1, D), b.reshape(1, D))


def _ffn_up_kernel(x_ref, wg_ref, wv_ref, cwg_ref, cwv_ref, cbg_ref, cbv_ref, o_ref,
                   carry_ref, *, ts, tiles_per_seq):
    i = pl.program_id(0)
    j = pl.program_id(1)
    nsub = x_ref.shape[0] // ts
    row = lax.broadcasted_iota(jnp.int32, (ts, 1), 0)
    wg = wg_ref[...].astype(BF16)
    wv = wv_ref[...].astype(BF16)

    def up(u):
        x = x_ref[u * ts:(u + 1) * ts, :]
        return (jnp.dot(x, wg, preferred_element_type=F32), jnp.dot(x, wv, preferred_element_type=F32))

    def conv(hcur, prev, cw_ref, cb_ref):
        p1 = prev[7:8, :]
        p2 = prev[6:7, :]
        h1 = jnp.where(row == 0, p1, pltpu.roll(hcur, 1, axis=0))
        h2 = jnp.where(row == 0, p2, jnp.where(row == 1, p1, pltpu.roll(hcur, 2, axis=0)))
        return cw_ref[0:1, :] * h2 + cw_ref[1:2, :] * h1 + cw_ref[2:3, :] * hcur + cb_ref[...]

    prev = (carry_ref[0, j], carry_ref[1, j])
    raw = up(0)
    for u in range(nsub):
        raw_next = up(u + 1) if u + 1 < nsub else None
        seq_start = ((i * nsub + u) % tiles_per_seq) == 0
        gate = conv(raw[0], jnp.where(seq_start, 0.0, prev[0]), cwg_ref, cbg_ref)
        val = conv(raw[1], jnp.where(seq_start, 0.0, prev[1]), cwv_ref, cbv_ref)
        o_ref[u * ts:(u + 1) * ts, :] = (gate * jax.nn.sigmoid(gate) * val).astype(o_ref.dtype)
        prev = (raw[0][ts - 8:, :], raw[1][ts - 8:, :])
        raw = raw_next
    carry_ref[0, j] = prev[0]
    carry_ref[1, j] = prev[1]


def _ffn_up(xb, w_up_all, layer, conv_w, conv_b, *, S):
    M, D = xb.shape
    F2 = w_up_all.shape[2]
    F = F2 // 2
    tn = 256
    assert F % tn == 0
    nf = F // tn
    ts = _pick(S, (1024, 512, 256, 128))
    tm = 2 * ts if M % (2 * ts) == 0 else ts
    conv_b = conv_b.reshape(1, F2)
    return pl.pallas_call(
        functools.partial(_ffn_up_kernel, ts=ts, tiles_per_seq=S // ts),
        out_shape=jax.ShapeDtypeStruct((M, F), BF16),
        grid=(M // tm, nf),
        in_specs=[pl.BlockSpec((tm, D), lambda i, j: (i, 0), pipeline_mode=pl.Buffered(1)),
                  pl.BlockSpec((None, D, tn), lambda i, j: (layer, 0, j)),
                  pl.BlockSpec((None, D, tn), lambda i, j: (layer, 0, nf + j)),
                  pl.BlockSpec((CONV_W, tn), lambda i, j: (0, j)),
                  pl.BlockSpec((CONV_W, tn), lambda i, j: (0, nf + j)),
                  pl.BlockSpec((1, tn), lambda i, j: (0, j)),
                  pl.BlockSpec((1, tn), lambda i, j: (0, nf + j))],
        out_specs=pl.BlockSpec((tm, tn), lambda i, j: (i, j)),
        scratch_shapes=[pltpu.VMEM((2, nf, 8, tn), F32)],
        compiler_params=_params(("arbitrary", "arbitrary")),
        name="ffn_up_conv_glu",
    )(xb, w_up_all, w_up_all, conv_w, conv_w, conv_b, conv_b)


def _mix_kernel(x_ref, h_ref, g_ref, b_ref, mix_ref, xo_ref, *rest, alpha, tiles_per_seq):
    outs, carry_ref = rest[:-1], rest[-1]
    i = pl.program_id(0)
    tm = x_ref.shape[0]
    x = _layer_norm(alpha * x_ref[...] + h_ref[...], g_ref[...], b_ref[...])
    xo_ref[...] = x
    row = lax.broadcasted_iota(jnp.int32, (tm, 1), 0)
    prev_last = jnp.where((i % tiles_per_seq) == 0, 0.0, carry_ref[7:8, :])
    xx = jnp.where(row == 0, prev_last, pltpu.roll(x, 1, axis=0)) - x
    carry_ref[...] = x[tm - 8:, :]
    for m, o_ref in enumerate(outs):
        o_ref[...] = (x + xx * mix_ref[m:m + 1, :]).astype(BF16)


def _norm_token_shift_mix(x, h, g, b, mix, alpha, *, S):
    M, D = x.shape
    tm = _pick(S, (128,))
    n = mix.shape[0]
    row = pl.BlockSpec((tm, D), lambda i: (i, 0))
    vec = pl.BlockSpec((1, D), lambda i: (0, 0))
    outs = pl.pallas_call(
        functools.partial(_mix_kernel, alpha=alpha, tiles_per_seq=S // tm),
        out_shape=(jax.ShapeDtypeStruct((M, D), F32),)
        + tuple(jax.ShapeDtypeStruct((M, D), BF16) for _ in range(n)),
        grid=(M // tm,),
        in_specs=[row, row, vec, vec, pl.BlockSpec((n, D), lambda i: (0, 0))],
        out_specs=(row,) + tuple(row for _ in range(n)),
        scratch_shapes=[pltpu.VMEM((8, D), F32)],
        compiler_params=_params(("arbitrary",)),
        name="norm_token_shift_mix",
    )(x, h, g.reshape(1, D), b.reshape(1, D), mix)
    return outs[0], outs[1:]


def _lora_kernel(x_ref, w1_ref, w2_ref, b_ref, o_ref, *, mode):
    t = jnp.dot(x_ref[...], w1_ref[...], preferred_element_type=F32)
    if mode == "decay":
        t = jnp.tanh(t)
    elif mode == "gate":
        t = jax.nn.sigmoid(t)
    z = jnp.dot(t.astype(BF16), w2_ref[...], preferred_element_type=F32) + b_ref[...]
    if mode == "decay":
        u = -z
        sp = jnp.maximum(u, 0.0) + jnp.log(1.0 + jnp.exp(-jnp.abs(u)))
        z = -jnp.exp(-sp - 0.5)
    elif mode == "lr":
        z = jax.nn.sigmoid(z)
    o_ref[...] = z


def _lora(xb, w1, w2, bias, mode):
    M, D = xb.shape
    R = w1.shape[1]
    tm = _pick(M, (512, 256, 128))
    row = pl.BlockSpec((tm, D), lambda i: (i, 0))
    return pl.pallas_call(
        functools.partial(_lora_kernel, mode=mode),
        out_shape=jax.ShapeDtypeStruct((M, D), F32),
        grid=(M // tm,),
        in_specs=[row, pl.BlockSpec((D, R), lambda i: (0, 0)),
                  pl.BlockSpec((R, D), lambda i: (0, 0)),
                  pl.BlockSpec((1, D), lambda i: (0, 0))],
        out_specs=row,
        compiler_params=_params(("parallel",)),
        name="lora_" + mode,
    )(xb, w1, w2, bias.reshape(1, D))


def _dot_bf(a, b):
    return jnp.dot(a.astype(BF16), b.astype(BF16), preferred_element_type=F32)


def _split3(x):
    hi = x.astype(BF16)
    r1 = x - hi.astype(F32)
    mid = r1.astype(BF16)
    lo = (r1 - mid.astype(F32)).astype(BF16)
    return hi, mid, lo


def _dot_exact_rhs(x, w_bf):
    n = x.shape[0]
    r = jnp.dot(jnp.concatenate(_split3(x), axis=0), w_bf, preferred_element_type=F32)
    return r[:n] + r[n:2 * n] + r[2 * n:]


def _dot_exact_lhs(w_bf, x):
    n = x.shape[1]
    r = jnp.dot(w_bf, jnp.concatenate(_split3(x), axis=1), preferred_element_type=F32)
    return r[:, :n] + r[:, n:2 * n] + r[:, 2 * n:]


def _dot_x3(a, b):
    n = a.shape[0]
    a_hi = a.astype(BF16)
    a_lo = (a - a_hi.astype(F32)).astype(BF16)
    b_hi = b.astype(BF16)
    b_lo = (b - b_hi.astype(F32)).astype(BF16)
    r = jnp.dot(jnp.concatenate([a_hi, a_lo], axis=0), b_hi, preferred_element_type=F32)
    return r[:n] + r[n:] + jnp.dot(a_hi, b_lo, preferred_element_type=F32)


def _head_ones():
    r = lax.broadcasted_iota(jnp.int32, (_LANES, _LANES), 0) // RW_HEAD
    c = lax.broadcasted_iota(jnp.int32, (_LANES, _LANES), 1) // RW_HEAD
    return jnp.where(r == c, 1.0, 0.0).astype(BF16)


def _rwkv_prep_kernel(r_ref, k_ref, v_ref, lw_ref, a_ref, kk_ref, ka_ref, rk_ref, lb_ref,
                      rp_ref, y0_ref, e_ref, g_ref, h_ref, *, npairs):
    L = RW_CHUNK
    n2 = 2 * L
    pairs = range(npairs)
    sls = [slice(p * _LANES, (p + 1) * _LANES) for p in pairs]
    ones_bd = _head_ones()
    ri = lax.broadcasted_iota(jnp.int32, (n2, n2), 0)
    ci = lax.broadcasted_iota(jnp.int32, (n2, n2), 1)
    strict = ci < ri
    incl = ci <= ri
    diag = ri == ci
    tri = jnp.where(lax.broadcasted_iota(jnp.int32, (L, L), 1)
                    <= lax.broadcasted_iota(jnp.int32, (L, L), 0), 1.0, 0.0).astype(BF16)
    head0 = lax.broadcasted_iota(jnp.int32, (L, _LANES), 1) < RW_HEAD

    def stack(x):
        return jnp.concatenate([jnp.where(head0, x, 0.0), jnp.where(head0, 0.0, x)], axis=0)

    def unstack(x):
        return x[:L] + x[L:]

    r = [r_ref[:, sl] for sl in sls]
    k = [k_ref[:, sl] for sl in sls]
    v = [v_ref[:, sl] for sl in sls]
    lw = [lw_ref[:, sl] for sl in sls]
    a = [a_ref[:, sl] for sl in sls]
    kk = [k[p] * kk_ref[:, sls[p]] for p in pairs]
    kmod = [k[p] * (1.0 + (a[p] - 1.0) * ka_ref[:, sls[p]]) for p in pairs]
    cl = [_dot_exact_lhs(tri, lw[p]) for p in pairs]
    hs = [_dot_exact_rhs(jnp.concatenate([kk[p] * kk[p], r[p] * kmod[p] * rk_ref[:, sls[p]]], axis=0),
                         ones_bd) for p in pairs]
    for p in pairs:
        e_ref[:, sls[p]] = lb_ref[:, sls[p]] + hs[p][L:] * v[p]
    kk = [kk[p] / jnp.maximum(jnp.sqrt(hs[p][:L]), 1e-12) for p in pairs]
    bv = [kk[p] * a[p] for p in pairs]
    cl_end = [c[L - 1:L, :] for c in cl]
    c_inv = [jnp.exp(-c) for c in cl]
    c_rem = [jnp.exp(cl_end[p] - cl[p]) for p in pairs]
    a_s = [stack(-kk[p] * jnp.exp(cl[p] - lw[p])) for p in pairs]
    r_s = [stack(r[p] * jnp.exp(cl[p])) for p in pairs]
    v_s = [stack(v[p]).astype(BF16) for p in pairs]
    bk_t = [jnp.concatenate([stack(bv[p] * c_inv[p]), stack(kmod[p] * c_inv[p])], axis=0)
            .astype(BF16).T for p in pairs]
    ar_bk = [_dot_bf(jnp.concatenate([a_s[p], r_s[p]], axis=0), bk_t[p]) for p in pairs]
    nab = [jnp.where(strict, m[:n2, :n2], 0.0) for m in ar_bk]
    nak = [jnp.where(strict, m[:n2, n2:], 0.0) for m in ar_bk]
    mrb = [jnp.where(incl, m[n2:, :n2], 0.0) for m in ar_bk]
    mrk = [jnp.where(incl, m[n2:, n2:], 0.0) for m in ar_bk]
    xv = [_dot_bf(jnp.concatenate([nak[p], mrk[p], stack(kmod[p] * c_rem[p]).T], axis=0), v_s[p])
          for p in pairs]
    levels = int(math.log2(L)) - 1
    t_inv = [jnp.where(diag, 1.0, m) for m in nab]
    x = [_dot_bf(m, m) for m in nab]
    for lvl in range(levels):
        if lvl < levels - 1:
            tx = [_dot_bf(jnp.concatenate([t_inv[p], x[p]], axis=0), x[p]) for p in pairs]
            t_inv = [t_inv[p] + tx[p][:n2] for p in pairs]
            x = [tx[p][n2:] for p in pairs]
        else:
            t_inv = [t_inv[p] + _dot_bf(t_inv[p], x[p]) for p in pairs]
    pq = [_dot_bf(t_inv[p], jnp.concatenate([a_s[p], xv[p][:n2]], axis=1)) for p in pairs]
    rg = [_dot_bf(jnp.concatenate([mrb[p], stack(bv[p] * c_rem[p]).T], axis=0), pq[p]) for p in pairs]
    for p in pairs:
        rp_ref[:, sls[p]] = unstack(r_s[p] + rg[p][:n2, :_LANES])
        y0_ref[:, sls[p]] = unstack(rg[p][:n2, _LANES:] + xv[p][n2:2 * n2])
        g_ref[p] = unstack(jnp.where(diag, jnp.exp(cl_end[p]), 0.0) + rg[p][n2:, :_LANES])
        h_ref[p] = unstack(rg[p][n2:, _LANES:] + xv[p][2 * n2:])


def _rwkv_scan_kernel(rp_ref, y0_ref, e_ref, gate_ref, g_ref, h_ref, lnw_ref, o_ref, st_ref, *, npairs):
    c = pl.program_id(2)

    @pl.when(c == 0)
    def _():
        st_ref[...] = jnp.zeros(st_ref.shape, F32)

    ones_bd = _head_ones()
    inv_n = 1.0 / RW_HEAD
    pairs = range(npairs)
    sls = [slice(p * _LANES, (p + 1) * _LANES) for p in pairs]
    head0 = lax.broadcasted_iota(jnp.int32, (RW_HEAD, _LANES), 1) < RW_HEAD

    def block_diag(x):
        return jnp.concatenate([jnp.where(head0, x, 0.0), jnp.where(head0, 0.0, x)], axis=0)

    prod = [_dot_x3(jnp.concatenate([block_diag(g_ref[p]), rp_ref[:, sls[p]]], axis=0), st_ref[p])
            for p in pairs]
    for p in pairs:
        st_ref[p] = prod[p][:_LANES] + block_diag(h_ref[p])
    y = [prod[p][_LANES:] + y0_ref[:, sls[p]] for p in pairs]
    mu = [_dot_exact_rhs(y[p], ones_bd) * inv_n for p in pairs]
    yc = [y[p] - mu[p] for p in pairs]
    var = [_dot_exact_rhs(yc[p] * yc[p], ones_bd) * inv_n for p in pairs]
    for p in pairs:
        yn = yc[p] * lax.rsqrt(var[p] + GN_EPS)
        o_ref[:, sls[p]] = ((yn * lnw_ref[:, sls[p]] + e_ref[:, sls[p]])
                            * gate_ref[:, sls[p]]).astype(o_ref.dtype)


def _rwkv_core(r, k, v, lw, a, gate, k_k, k_a, r_k, lnx_w, lnx_b, *, B):
    M, D = r.shape
    S = M // B
    L = RW_CHUNK
    nc = S // L
    ngroups = D // _LANES
    np1 = _pick(ngroups, (16, 8, 4, 2, 1))
    np2 = _pick(ngroups, (16, 8, 4, 2, 1))
    vec = lambda t: t.astype(F32).reshape(1, D)

    w1 = np1 * _LANES
    tile1 = pl.BlockSpec((L, w1), lambda b, c, g: (b * nc + c, g))
    par1 = pl.BlockSpec((1, w1), lambda b, c, g: (0, g))
    mat1 = pl.BlockSpec((None, None, np1, RW_HEAD, _LANES), lambda b, c, g: (b, c, g, 0, 0))
    md = jax.ShapeDtypeStruct((M, D), F32)
    gh = jax.ShapeDtypeStruct((B, nc, ngroups, RW_HEAD, _LANES), F32)
    rp, y0, e, gm, hm = pl.pallas_call(
        functools.partial(_rwkv_prep_kernel, npairs=np1),
        out_shape=(md, md, md, gh, gh),
        grid=(B, nc, ngroups // np1),
        in_specs=[tile1] * 5 + [par1] * 4,
        out_specs=(tile1, tile1, tile1, mat1, mat1),
        compiler_params=_params(("parallel", "parallel", "parallel")),
        name="rwkv_chunk_prep",
    )(r, k, v, lw, a, vec(k_k), vec(k_a), vec(r_k), vec(lnx_b))

    w2 = np2 * _LANES
    tile2 = pl.BlockSpec((L, w2), lambda b, g, c: (b * nc + c, g))
    par2 = pl.BlockSpec((1, w2), lambda b, g, c: (0, g))
    mat2 = pl.BlockSpec((None, None, np2, RW_HEAD, _LANES), lambda b, g, c: (b, c, g, 0, 0))
    return pl.pallas_call(
        functools.partial(_rwkv_scan_kernel, npairs=np2),
        out_shape=jax.ShapeDtypeStruct((M, D), BF16),
        grid=(B, ngroups // np2, nc),
        in_specs=[tile2] * 4 + [mat2, mat2, par2],
        out_specs=tile2,
        scratch_shapes=[pltpu.VMEM((np2, _LANES, _LANES), F32)],
        compiler_params=_params(("parallel", "parallel", "arbitrary")),
        name="rwkv_state_scan",
    )(rp, y0, e, gate, gm, hm, vec(lnx_w))


def kernel(x, mem, positions, rel_bias, da_w_qkv, da_lam, da_subln, da_w_o, rw_mix, rw_w_rkv, rw_w0, rw_w1, rw_w2, rw_a0, rw_a1, rw_a2, rw_g1, rw_g2, rw_k_k, rw_k_a, rw_r_k, rw_lnx_w, rw_lnx_b, rw_w_o, ca_w_q, ca_w_kv, ca_w_o, ffn_w_up, ffn_conv_w, ffn_conv_b, ffn_w_down, ln_g, ln_b):
    B, S, D = x.shape
    M = B * S
    depth = ln_g.shape[0]
    alpha = (2 * depth) ** 0.25
    bf = lambda t: t.astype(BF16)
    memb = bf(mem).reshape(B * mem.shape[1], D)
    F = ffn_w_down.shape[1]
    tk_down = F // 2 if (F // 2) % _LANES == 0 else F

    xf = x.reshape(M, D)
    pending = None
    for i in range(depth):
        j = i // 2
        if i % 2 == 0:
            if pending is None:
                xb = bf(xf)
            else:
                xf, xb = _deepnorm(xf, *pending, alpha, with_bf16=True)
            lam_init = 0.8 - 0.6 * math.exp(-0.3 * i)
            qkv = _matmul_f32w(xb, da_w_qkv, (j,), BF16, scaled_cols=D,
                               col_scale=DA_HEAD_DIM ** -0.5 * _LOG2E)
            att = _diff_attention(qkv.reshape(B, S, 3 * D), positions, rel_bias, da_lam[j],
                                  da_subln[j], lam_init)
            h = _matmul_f32w(att.reshape(M, D), da_w_o, (j,), F32)
        else:
            xf, (xr, xw, xk, xv, xa, xg) = _norm_token_shift_mix(
                xf, *pending, rw_mix[j].astype(F32), alpha, S=S)
            r = _matmul_f32w(xr, rw_w_rkv, (j, 0), F32)
            k = _matmul_f32w(xk, rw_w_rkv, (j, 1), F32)
            v = _matmul_f32w(xv, rw_w_rkv, (j, 2), F32)
            lw = _lora(xw, bf(rw_w1[j]), bf(rw_w2[j]), rw_w0[j].astype(F32), "decay")
            a = _lora(xa, bf(rw_a1[j]), bf(rw_a2[j]), rw_a0[j].astype(F32), "lr")
            g = _lora(xg, bf(rw_g1[j]), bf(rw_g2[j]), jnp.zeros((D,), F32), "gate")
            o = _rwkv_core(r, k, v, lw, a, g, rw_k_k[j], rw_k_a[j], rw_r_k[j].reshape(D),
                           rw_lnx_w[j], rw_lnx_b[j], B=B)
            h = _matmul_f32w(o, rw_w_o, (j,), F32)

        kv = _matmul(memb, bf(ca_w_kv[i]), BF16).reshape(B, mem.shape[1], 2 * CA_DIM)
        xf, xb = _norm_cross_attention(xf, h, ln_g[i, 0], ln_b[i, 0], kv, bf(ca_w_q[i]),
                                       bf(ca_w_o[i]), ln_g[i, 1], ln_b[i, 1], alpha, B=B)

        act = _ffn_up(xb, ffn_w_up.astype(F32), i, ffn_conv_w[i].astype(F32), ffn_conv_b[i].astype(F32), S=S)
        h = _matmul(act, bf(ffn_w_down[i]), F32, tn=512, tk=tk_down)
        pending = (h, ln_g[i, 2], ln_b[i, 2])
    (xf,) = _deepnorm(xf, *pending, alpha, with_bf16=False)
    return xf.reshape(B, S, D)
```
